```python
import jax, jax.numpy as jnp
from jax import lax
import numpy as np

D_MODEL = 4096
BATCH = 32
SEQ = 256
DEPTH = 2
DEC_BATCH = 8
DEC_SEQ = 4096
PAST_LEN = 256

GRID_W = 64
CHUNK = 128
BLOCK_Q = 128
N_EVEN = (DEPTH + 1) // 2
N_ODD = DEPTH // 2
A_GROUPS = 16
A_DIM = 128
A_WIDTH = A_GROUPS * A_DIM
MLA_HEADS = 16
MLA_NOPE = 128
MLA_ROPE = 64
MLA_V = 128
Q_LORA = 1024
KV_LORA = 512
MLA_SCALE = (MLA_NOPE + MLA_ROPE) ** -0.5
SWA_HEADS = 32
SWA_KV_HEADS = 8
SWA_GROUP = SWA_HEADS // SWA_KV_HEADS
HEAD_DIM = 128
WINDOW = 128
SWA_SCALE = HEAD_DIM ** -0.5
D_FF = 11008
MACARON = 0.5
ROPE_BASE = 10000.0
EPS = 1e-6
N_MOD = 9
AB_IN = 2 * A_WIDTH + Q_LORA + KV_LORA + MLA_ROPE
AB_OUT = A_WIDTH + MLA_HEADS * MLA_V
C_KV = SWA_KV_HEADS * HEAD_DIM
C_OUT = SWA_HEADS * HEAD_DIM
C_IN = C_OUT + 2 * C_KV

kernel_name = 'hybrid_flow_trunk_step'


def rms_norm(x, g):
    xf = x.astype(jnp.float32)
    y = xf * lax.rsqrt(jnp.mean(xf * xf, axis=-1, keepdims=True) + EPS)
    return (y * g.astype(jnp.float32)).astype(x.dtype)


def grid_rope_tables(rows, rot_dim):
    pos_r = jnp.repeat(jnp.arange(rows, dtype=jnp.float32), GRID_W)
    pos_c = jnp.tile(jnp.arange(GRID_W, dtype=jnp.float32), rows)
    half = rot_dim // 2
    inv_freq = 1.0 / (ROPE_BASE ** (jnp.arange(0, half, 2, dtype=jnp.float32) / half))
    ang_r = pos_r[:, None] * inv_freq[None, :]
    ang_c = pos_c[:, None] * inv_freq[None, :]
    return (jnp.cos(ang_r), jnp.sin(ang_r), jnp.cos(ang_c), jnp.sin(ang_c))


def _rotate(x, cos, sin):
    x1, x2 = jnp.split(x, 2, axis=-1)
    c = cos[None, :, None, :].astype(x.dtype)
    s = sin[None, :, None, :].astype(x.dtype)
    return jnp.concatenate([x1 * c - x2 * s, x2 * c + x1 * s], axis=-1)


def apply_grid_rope(x, tables):
    cr, sr, cc, sc = tables
    xr, xc = jnp.split(x, 2, axis=-1)
    return jnp.concatenate([_rotate(xr, cr, sr), _rotate(xc, cc, sc)], axis=-1)


def ada_mod(cond, w_mod, b_mod):
    m = jax.nn.silu(cond) @ w_mod + b_mod
    return m.reshape(m.shape[0], N_MOD, D_MODEL)


def modulated_input(x, mods, g_pre, base):
    return rms_norm(x, g_pre) * (1 + mods[:, base + 1, None, :]) + mods[:, base, None, :]


def gated_residual(x, y, mods, g_post, base, weight):
    return x + weight * mods[:, base + 2, None, :] * rms_norm(y, g_post)


def ffn_sublayer(x, mods, g_pre, g_post, w_gate, w_up, w_down, base):
    h = modulated_input(x, mods, g_pre, base)
    y = (jax.nn.silu(h @ w_gate) * (h @ w_up)) @ w_down
    return gated_residual(x, y, mods, g_post, base, MACARON)


def chunk_spatial_gate(u, v, w_s, b_s):
    b, n, _ = v.shape
    vf = v.astype(jnp.float32)
    mu = jnp.mean(vf, axis=-1, keepdims=True)
    var = jnp.mean(jnp.square(vf - mu), axis=-1, keepdims=True)
    vn = ((vf - mu) * lax.rsqrt(var + EPS)).astype(v.dtype)
    vc = vn.reshape(b, n // CHUNK, CHUNK, A_GROUPS, A_DIM)
    mixed = jnp.einsum('gpq,bcqgd->bcpgd', w_s, vc) + jnp.transpose(b_s)[:, :, None]
    return u * mixed.reshape(b, n, A_WIDTH)


def ab_project(h, w_in, g_q, g_kv, w_uq):
    b, n, _ = h.shape
    z = h @ w_in
    o1, o2 = A_WIDTH, 2 * A_WIDTH
    o3 = o2 + Q_LORA
    o4 = o3 + KV_LORA
    u = jax.nn.gelu(z[..., :o1])
    v = jax.nn.gelu(z[..., o1:o2])
    q = (rms_norm(z[..., o2:o3], g_q) @ w_uq).reshape(b, n, MLA_HEADS, MLA_NOPE + MLA_ROPE)
    ckv = rms_norm(z[..., o3:o4], g_kv)
    kr = z[..., o4:]
    return u, v, q[..., :MLA_NOPE], q[..., MLA_NOPE:], ckv, kr


def mla_up(ckv, w_ukv):
    b, n, _ = ckv.shape
    kv = (ckv @ w_ukv).reshape(b, n, MLA_HEADS, MLA_NOPE + MLA_V)
    return kv[..., :MLA_NOPE], kv[..., MLA_NOPE:]


def mla_attend(q_nope, q_rope, k_nope, k_rope, v):
    s = (jnp.einsum('bqhd,bkhd->bhqk', q_nope, k_nope, preferred_element_type=jnp.float32)
         + jnp.einsum('bqhr,bkr->bhqk', q_rope, k_rope, preferred_element_type=jnp.float32)) * MLA_SCALE
    p = jax.nn.softmax(s, axis=-1).astype(v.dtype)
    return jnp.einsum('bhqk,bkhd->bqhd', p, v)


def mixer_ab_context(h, w_in, g_q, g_kv, w_uq, w_ukv, w_s, b_s, w_out):
    b, n, _ = h.shape
    u, v, qn, qr, ckv, kr = ab_project(h, w_in, g_q, g_kv, w_uq)
    a = chunk_spatial_gate(u, v, w_s, b_s)
    kn, vv = mla_up(ckv, w_ukv)
    o = mla_attend(qn, qr, kn, kr, vv).reshape(b, n, MLA_HEADS * MLA_V)
    return jnp.concatenate([a, o], axis=-1) @ w_out, ckv, kr


def mixer_ab_latent(h, ckv_ctx, kr_ctx, tables, w_in, g_q, g_kv, w_uq, w_ukv, w_s, b_s, w_out):
    b, n, _ = h.shape
    u, v, qn, qr, ckv, kr = ab_project(h, w_in, g_q, g_kv, w_uq)
    a = chunk_spatial_gate(u, v, w_s, b_s)
    qr = apply_grid_rope(qr, tables)
    kr = apply_grid_rope(kr[:, :, None, :], tables)[:, :, 0, :]
    kn_l, v_l = mla_up(ckv, w_ukv)
    kn_c, v_c = mla_up(ckv_ctx.astype(h.dtype), w_ukv)
    kn = jnp.concatenate([kn_l, kn_c], axis=1)
    kr_all = jnp.concatenate([kr, kr_ctx.astype(h.dtype)], axis=1)
    v_all = jnp.concatenate([v_l, v_c], axis=1)

    def block(i):
        st = i * BLOCK_Q
        qn_b = lax.dynamic_slice_in_dim(qn, st, BLOCK_Q, axis=1)
        qr_b = lax.dynamic_slice_in_dim(qr, st, BLOCK_Q, axis=1)
        return mla_attend(qn_b, qr_b, kn, kr_all, v_all)

    o = lax.map(block, jnp.arange(n // BLOCK_Q))
    o = jnp.moveaxis(o, 0, 1).reshape(b, n, MLA_HEADS * MLA_V)
    return jnp.concatenate([a, o], axis=-1) @ w_out


def c_project(h, w_in):
    b, n, _ = h.shape
    z = h @ w_in
    q = z[..., :C_OUT].reshape(b, n, SWA_KV_HEADS, SWA_GROUP, HEAD_DIM)
    k = z[..., C_OUT:C_OUT + C_KV].reshape(b, n, SWA_KV_HEADS, HEAD_DIM)
    v = z[..., C_OUT + C_KV:].reshape(b, n, SWA_KV_HEADS, HEAD_DIM)
    return q, k, v


def swa_context(h, w_in, sink, w_out):
    b, n, _ = h.shape
    q, k, v = c_project(h, w_in)
    s = jnp.einsum('bqhgd,bkhd->bhgqk', q, k, preferred_element_type=jnp.float32) * SWA_SCALE
    sk = jnp.broadcast_to(sink.astype(jnp.float32).reshape(1, SWA_KV_HEADS, SWA_GROUP, 1, 1), s.shape[:-1] + (1,))
    p = jax.nn.softmax(jnp.concatenate([sk, s], axis=-1), axis=-1)[..., 1:].astype(v.dtype)
    o = jnp.einsum('bhgqk,bkhd->bqhgd', p, v).reshape(b, n, C_OUT)
    return o @ w_out, k, v


def swa_latent(h, k_ctx, v_ctx, tables, w_in, sink, w_out):
    b, n, _ = h.shape
    q, k, v = c_project(h, w_in)
    q = apply_grid_rope(q.reshape(b, n, SWA_HEADS, HEAD_DIM), tables).reshape(b, n, SWA_KV_HEADS, SWA_GROUP, HEAD_DIM)
    k = apply_grid_rope(k, tables)
    k_ctx = k_ctx.astype(h.dtype)
    v_ctx = v_ctx.astype(h.dtype)
    pad = ((0, 0), (WINDOW, WINDOW), (0, 0), (0, 0))
    kp = jnp.pad(k, pad)
    vp = jnp.pad(v, pad)
    span = BLOCK_Q + 2 * WINDOW
    pidx = jnp.arange(n + 2 * WINDOW)
    key_valid = (pidx >= WINDOW) & (pidx < n + WINDOW)
    rel = jnp.arange(span)[None, :] - jnp.arange(BLOCK_Q)[:, None]
    band = (rel >= 0) & (rel <= 2 * WINDOW)
    sink_b = sink.astype(jnp.float32).reshape(1, SWA_KV_HEADS, SWA_GROUP, 1, 1)
    neg = jnp.finfo(jnp.float32).min

    def block(i):
        st = i * BLOCK_Q
        qb = lax.dynamic_slice_in_dim(q, st, BLOCK_Q, axis=1)
        kb = lax.dynamic_slice_in_dim(kp, st, span, axis=1)
        vb = lax.dynamic_slice_in_dim(vp, st, span, axis=1)
        mask = band & lax.dynamic_slice_in_dim(key_valid, st, span, axis=0)[None, :]
        s_loc = jnp.einsum('bqhgd,bkhd->bhgqk', qb, kb, preferred_element_type=jnp.float32) * SWA_SCALE
        s_loc = jnp.where(mask, s_loc, neg)
        s_ctx = jnp.einsum('bqhgd,bkhd->bhgqk', qb, k_ctx, preferred_element_type=jnp.float32) * SWA_SCALE
        sk = jnp.broadcast_to(sink_b, s_loc.shape[:-1] + (1,))
        p = jax.nn.softmax(jnp.concatenate([sk, s_loc, s_ctx], axis=-1), axis=-1)
        p_loc = p[..., 1:1 + span].astype(v.dtype)
        p_ctx = p[..., 1 + span:].astype(v.dtype)
        return (jnp.einsum('bhgqk,bkhd->bqhgd', p_loc, vb)
                + jnp.einsum('bhgqk,bkhd->bqhgd', p_ctx, v_ctx))

    o = lax.map(block, jnp.arange(n // BLOCK_Q))
    o = jnp.moveaxis(o, 0, 1).reshape(b, n, C_OUT)
    return o @ w_out


def setup_inputs(seed: int = 0) -> dict:
    key = jax.random.key(seed)
    ks = jax.random.split(key, 32)
    nrm = lambda k, shape, scale: jax.random.normal(k, shape, jnp.float32) * scale
    gain = lambda k, shape: 1.0 + 0.02 * jax.random.normal(k, shape, jnp.float32)
    return {
        'x_prompt': nrm(ks[0], (BATCH, SEQ, D_MODEL), 1.0),
        'x_sample': nrm(ks[1], (DEC_BATCH, DEC_SEQ, D_MODEL), 1.0),
        'cache_mla_ckv': nrm(ks[2], (DEC_BATCH, N_EVEN, PAST_LEN, KV_LORA), 1.0),
        'cache_mla_krope': nrm(ks[3], (DEC_BATCH, N_EVEN, PAST_LEN, MLA_ROPE), 1.0),
        'cache_swa_k': nrm(ks[4], (DEC_BATCH, N_ODD, PAST_LEN, SWA_KV_HEADS, HEAD_DIM), 1.0),
        'cache_swa_v': nrm(ks[5], (DEC_BATCH, N_ODD, PAST_LEN, SWA_KV_HEADS, HEAD_DIM), 1.0),
        'c': nrm(ks[6], (DEC_BATCH, D_MODEL), 1.0),
        'c_ctx': nrm(ks[7], (D_MODEL,), 1.0),
        'w_mod': nrm(ks[8], (DEPTH, D_MODEL, N_MOD * D_MODEL), D_MODEL ** -0.5),
        'b_mod': nrm(ks[9], (DEPTH, N_MOD * D_MODEL), 0.01),
        'norm_pre': gain(ks[10], (DEPTH, 3, D_MODEL)),
        'norm_post': gain(ks[11], (DEPTH, 3, D_MODEL)),
        'w_ffn_gate': nrm(ks[12], (DEPTH, 2, D_MODEL, D_FF), D_MODEL ** -0.5),
        'w_ffn_up': nrm(ks[13], (DEPTH, 2, D_MODEL, D_FF), D_MODEL ** -0.5),
        'w_ffn_down': nrm(ks[14], (DEPTH, 2, D_FF, D_MODEL), D_FF ** -0.5),
        'w_in_ab': nrm(ks[15], (N_EVEN, D_MODEL, AB_IN), D_MODEL ** -0.5),
        'g_q': gain(ks[16], (N_EVEN, Q_LORA)),
        'g_kv': gain(ks[17], (N_EVEN, KV_LORA)),
        'w_uq': nrm(ks[18], (N_EVEN, Q_LORA, MLA_HEADS * (MLA_NOPE + MLA_ROPE)), Q_LORA ** -0.5),
        'w_ukv': nrm(ks[19], (N_EVEN, KV_LORA, MLA_HEADS * (MLA_NOPE + MLA_V)), KV_LORA ** -0.5),
        'w_s': nrm(ks[20], (N_EVEN, A_GROUPS, CHUNK, CHUNK), CHUNK ** -0.5),
        'b_s': gain(ks[21], (N_EVEN, A_GROUPS, CHUNK)),
        'w_out_ab': nrm(ks[22], (N_EVEN, AB_OUT, D_MODEL), AB_OUT ** -0.5),
        'w_in_c': nrm(ks[23], (N_ODD, D_MODEL, C_IN), D_MODEL ** -0.5),
        'sink_c': nrm(ks[24], (N_ODD, SWA_HEADS), 0.5),
        'w_out_c': nrm(ks[25], (N_ODD, C_OUT, D_MODEL), C_OUT ** -0.5),
    }


def reference(x_prompt, x_sample, cache_mla_ckv, cache_mla_krope, cache_swa_k, cache_swa_v, c, c_ctx,
              w_mod, b_mod, norm_pre, norm_post, w_ffn_gate, w_ffn_up, w_ffn_down,
              w_in_ab, g_q, g_kv, w_uq, w_ukv, w_s, b_s, w_out_ab, w_in_c, sink_c, w_out_c):
    x = x_prompt
    ckv_l, kr_l, k_l, v_l = [], [], [], []
    for l in range(DEPTH):
        j = l // 2
        mods = ada_mod(c_ctx[None, :], w_mod[l], b_mod[l])
        x = ffn_sublayer(x, mods, norm_pre[l, 0], norm_post[l, 0],
                         w_ffn_gate[l, 0], w_ffn_up[l, 0], w_ffn_down[l, 0], 0)
        h = modulated_input(x, mods, norm_pre[l, 1], 3)
        if l % 2 == 0:
            y, ckv, kr = mixer_ab_context(h, w_in_ab[j], g_q[j], g_kv[j], w_uq[j], w_ukv[j],
                                          w_s[j], b_s[j], w_out_ab[j])
            ckv_l.append(ckv)
            kr_l.append(kr)
        else:
            y, k, v = swa_context(h, w_in_c[j], sink_c[j], w_out_c[j])
            k_l.append(k)
            v_l.append(v)
        x = gated_residual(x, y, mods, norm_post[l, 1], 3, 1.0)
        x = ffn_sublayer(x, mods, norm_pre[l, 2], norm_post[l, 2],
                         w_ffn_gate[l, 1], w_ffn_up[l, 1], w_ffn_down[l, 1], 6)
    y_prompt = x
    new_mla_ckv = jnp.stack(ckv_l, axis=1)
    new_mla_krope = jnp.stack(kr_l, axis=1)
    new_swa_k = jnp.stack(k_l, axis=1)
    new_swa_v = jnp.stack(v_l, axis=1)

    n_lat = x_sample.shape[1]
    rows = n_lat // GRID_W
    tab_mla = grid_rope_tables(rows, MLA_ROPE)
    tab_swa = grid_rope_tables(rows, HEAD_DIM)
    x = x_sample
    for l in range(DEPTH):
        j = l // 2
        mods = ada_mod(c, w_mod[l], b_mod[l])
        x = ffn_sublayer(x, mods, norm_pre[l, 0], norm_post[l, 0],
                         w_ffn_gate[l, 0], w_ffn_up[l, 0], w_ffn_down[l, 0], 0)
        h = modulated_input(x, mods, norm_pre[l, 1], 3)
        if l % 2 == 0:
            y = mixer_ab_latent(h, cache_mla_ckv[:, j], cache_mla_krope[:, j], tab_mla,
                                w_in_ab[j], g_q[j], g_kv[j], w_uq[j], w_ukv[j],
                                w_s[j], b_s[j], w_out_ab[j])
        else:
            y = swa_latent(h, cache_swa_k[:, j], cache_swa_v[:, j], tab_swa,
                           w_in_c[j], sink_c[j], w_out_c[j])
        x = gated_residual(x, y, mods, norm_post[l, 1], 3, 1.0)
        x = ffn_sublayer(x, mods, norm_pre[l, 2], norm_post[l, 2],
                         w_ffn_gate[l, 1], w_ffn_up[l, 1], w_ffn_down[l, 1], 6)
    y_sample = x
    return (y_prompt, y_sample, new_mla_ckv, new_mla_krope, new_swa_k, new_swa_v)
```

```python
import functools

import jax
import jax.numpy as jnp
from jax import lax
from jax.experimental import pallas as pl
from jax.experimental.pallas import tpu as pltpu

F32 = jnp.float32
BF16 = jnp.bfloat16

GRID_W = 64
CHUNK = 128
A_GROUPS = 16
A_DIM = 128
A_WIDTH = A_GROUPS * A_DIM
MLA_HEADS = 16
MLA_NOPE = 128
MLA_ROPE = 64
MLA_V = 128
Q_LORA = 1024
KV_LORA = 512
MLA_SCALE = (MLA_NOPE + MLA_ROPE) ** -0.5
SWA_HEADS = 32
SWA_KV_HEADS = 8
SWA_GROUP = SWA_HEADS // SWA_KV_HEADS
HEAD_DIM = 128
WINDOW = 128
SWA_SCALE = HEAD_DIM ** -0.5
MACARON = 0.5
ROPE_BASE = 10000.0
EPS = 1e-6
N_MOD = 9
C_KV = SWA_KV_HEADS * HEAD_DIM
C_OUT = SWA_HEADS * HEAD_DIM

LANES = 128
MXU_DEPTH = 256
V7X_VMEM_BYTES = 64 * 1024 * 1024
VMEM_REQUEST_CAP = V7X_VMEM_BYTES - 6 * 1024 * 1024
MOD_ROWS = 16
FF_PAD = 1024
HALF_GU = 512


def _pick(n, prefs):
    for p in prefs:
        if p <= n and n % p == 0:
            return p
    return n


def _params(semantics, block_bytes, temp_bytes=0):
    need = 2 * block_bytes + temp_bytes + (2 << 20)
    return pltpu.CompilerParams(dimension_semantics=semantics,
                                vmem_limit_bytes=int(min(max(need, 16 << 20), VMEM_REQUEST_CAP)))


def _rms(x, g):
    return x * lax.rsqrt(jnp.mean(x * x, axis=-1, keepdims=True) + EPS) * g


def _mod_kernel(c_ref, w_ref, b_ref, o_ref):
    c = c_ref[...]
    s = (c * jax.nn.sigmoid(c)).astype(BF16)
    o_ref[...] = jnp.dot(s, w_ref[...].astype(BF16), preferred_element_type=F32) + b_ref[...]


def _ada_mods(cond, w_mod, b_mod):
    depth, d, n = w_mod.shape
    tn = _pick(n, (1024, 512, 256, 128))
    out = pl.pallas_call(
        _mod_kernel,
        grid=(depth, n // tn),
        in_specs=[pl.BlockSpec((MOD_ROWS, d), lambda l, j: (0, 0)),
                  pl.BlockSpec((None, d, tn), lambda l, j: (l, 0, j)),
                  pl.BlockSpec((None, 1, tn), lambda l, j: (l, 0, j))],
        out_specs=pl.BlockSpec((None, MOD_ROWS, tn), lambda l, j: (l, 0, j)),
        out_shape=jax.ShapeDtypeStruct((depth, MOD_ROWS, n), F32),
        compiler_params=_params(("arbitrary", "arbitrary"), MOD_ROWS * d * 4 + d * tn * 4 + MOD_ROWS * tn * 4,
                                d * tn * 2),
        name="ada_mods",
    )(cond, w_mod, b_mod.reshape(depth, 1, n))
    return out.reshape(depth, MOD_ROWS, N_MOD, d)


def _pre_kernel(x_ref, m_ref, g_ref, h_ref, *, base):
    n = _rms(x_ref[...], g_ref[...])
    h_ref[...] = (n * (1.0 + m_ref[base + 1:base + 2, :]) + m_ref[base:base + 1, :]).astype(BF16)


def _post_kernel(x_ref, y_ref, mc_ref, gp_ref, *rest, gate, weight, base_next):
    x = x_ref[...] + weight * mc_ref[gate:gate + 1, :] * _rms(y_ref[...], gp_ref[...])
    if base_next is None:
        (xo_ref,) = rest
    else:
        mn_ref, gn_ref, xo_ref, h_ref = rest
        n = _rms(x, gn_ref[...])
        h_ref[...] = (n * (1.0 + mn_ref[base_next + 1:base_next + 2, :])
                      + mn_ref[base_next:base_next + 1, :]).astype(BF16)
    xo_ref[...] = x


def _row_specs(t, d, tm, rows_per_mod, row0):
    row = pl.BlockSpec((tm, d), lambda i: (i, 0))
    mod = pl.BlockSpec((None, N_MOD, d), lambda i: (row0 + (i * tm) // rows_per_mod, 0, 0))
    gain = pl.BlockSpec((1, d), lambda i: (0, 0))
    return row, mod, gain


def _pre(x, mods, g_pre, base, rows_per_mod, row0):
    t, d = x.shape
    tm = _pick(min(t, rows_per_mod), (256, 128))
    row, mod, gain = _row_specs(t, d, tm, rows_per_mod, row0)
    return pl.pallas_call(
        functools.partial(_pre_kernel, base=base),
        grid=(t // tm,),
        in_specs=[row, mod, gain],
        out_specs=row,
        out_shape=jax.ShapeDtypeStruct((t, d), BF16),
        compiler_params=_params(("arbitrary",), tm * d * 6 + N_MOD * d * 8, 3 * tm * d * 4),
        name="pre_norm",
    )(x, mods, g_pre.reshape(1, d))


def _post(x, y, mods_cur, g_post, gate, weight, rows_per_mod, row0, mods_next=None, g_next=None, base_next=None):
    t, d = x.shape
    tm = _pick(min(t, rows_per_mod), (256, 128))
    row, mod, gain = _row_specs(t, d, tm, rows_per_mod, row0)
    kern = functools.partial(_post_kernel, gate=gate, weight=weight, base_next=base_next)
    if base_next is None:
        return pl.pallas_call(
            kern, grid=(t // tm,),
            in_specs=[row, row, mod, gain],
            out_specs=row,
            out_shape=jax.ShapeDtypeStruct((t, d), F32),
            compiler_params=_params(("arbitrary",), 3 * tm * d * 4 + N_MOD * d * 4, 3 * tm * d * 4),
            name="post_norm",
        )(x, y, mods_cur, g_post.reshape(1, d)), None
    return pl.pallas_call(
        kern, grid=(t // tm,),
        in_specs=[row, row, mod, gain, mod, gain],
        out_specs=[row, row],
        out_shape=[jax.ShapeDtypeStruct((t, d), F32), jax.ShapeDtypeStruct((t, d), BF16)],
        compiler_params=_params(("arbitrary",), 3 * tm * d * 4 + tm * d * 2 + 2 * N_MOD * d * 4, 4 * tm * d * 4),
        name="post_pre_norm",
    )(x, y, mods_cur, g_post.reshape(1, d), mods_next, g_next.reshape(1, d))


def _rope(x, cos_ref, s_up_ref, s_dn_ref, shift):
    return (x * cos_ref[...] + pltpu.roll(x, LANES - shift, 1) * s_up_ref[...]
            + pltpu.roll(x, shift, 1) * s_dn_ref[...])


def _mm_kernel(*refs, n_in, epilogue, rope_shift, rope_groups, rope_blocks):
    a_refs, b_refs = refs[:n_in], refs[n_in:2 * n_in]
    o_ref = refs[-1]
    acc = jnp.dot(a_refs[0][...], b_refs[0][...], preferred_element_type=F32)
    for a_ref, b_ref in zip(a_refs[1:], b_refs[1:]):
        acc = acc + jnp.dot(a_ref[...], b_ref[...], preferred_element_type=F32)
    if epilogue == "gelu":
        o_ref[...] = jax.nn.gelu(acc, approximate=True).astype(o_ref.dtype)
    elif epilogue == "swiglu":
        half = acc.shape[1] // 2
        g = acc[:, :half]
        o_ref[...] = (g * jax.nn.sigmoid(g) * acc[:, half:]).astype(o_ref.dtype)
    elif epilogue == "rope":
        cos_ref, s_up_ref, s_dn_ref = refs[2 * n_in:2 * n_in + 3]
        groups = acc.shape[1] // LANES

        def roped():
            for c in range(groups):
                x = acc[:, c * LANES:(c + 1) * LANES]
                if rope_groups == "all" or c % 2 == 1:
                    x = _rope(x, cos_ref, s_up_ref, s_dn_ref, rope_shift)
                o_ref[:, c * LANES:(c + 1) * LANES] = x.astype(o_ref.dtype)

        if rope_blocks is None:
            roped()
        else:
            j = pl.program_id(1)
            pl.when(j < rope_blocks)(roped)

            @pl.when(j >= rope_blocks)
            def _():
                o_ref[...] = acc.astype(o_ref.dtype)
    else:
        o_ref[...] = acc.astype(o_ref.dtype)


def _matmul(a_list, b_list, out_dtype, epilogue=None, tables=None, rope_shift=None, rope_groups="all",
            rope_cols=None, name="matmul"):
    m = a_list[0].shape[0]
    n = b_list[0].shape[1]
    tm = _pick(m, (1024, 512, 256, 128))
    tn = _pick(n, (1024, 512, 256, 128))
    if tables is not None:
        tm = min(tm, tables[0].shape[0])
    tn_out = tn // 2 if epilogue == "swiglu" else tn
    n_out = n // 2 if epilogue == "swiglu" else n
    in_specs = [pl.BlockSpec((tm, a.shape[1]), lambda i, j: (i, 0)) for a in a_list]
    in_specs += [pl.BlockSpec((b.shape[0], tn), lambda i, j: (0, j)) for b in b_list]
    args = list(a_list) + list(b_list)
    k_total = sum(a.shape[1] for a in a_list)
    block_bytes = (tm * k_total + k_total * tn) * 2 + tm * tn_out * jnp.dtype(out_dtype).itemsize
    rope_blocks = None
    if epilogue == "rope":
        pos_blocks = tables[0].shape[0] // tm
        in_specs += [pl.BlockSpec((tm, LANES), lambda i, j: (i % pos_blocks, 0))] * 3
        args += list(tables)
        block_bytes += 3 * tm * LANES * 4
        if rope_cols is not None and rope_cols < n:
            assert rope_cols % tn == 0
            rope_blocks = rope_cols // tn
    kern = functools.partial(_mm_kernel, n_in=len(a_list), epilogue=epilogue, rope_shift=rope_shift,
                             rope_groups=rope_groups, rope_blocks=rope_blocks)
    return pl.pallas_call(
        kern, grid=(m // tm, n // tn),
        in_specs=in_specs,
        out_specs=pl.BlockSpec((tm, tn_out), lambda i, j: (i, j)),
        out_shape=jax.ShapeDtypeStruct((m, n_out), out_dtype),
        compiler_params=_params(("arbitrary", "arbitrary"), block_bytes, 3 * tm * tn * 4),
        name=name,
    )(*args)


def _mmk_kernel(a_ref, b_ref, o_ref):
    p = jnp.dot(a_ref[...], b_ref[...], preferred_element_type=F32)
    k = pl.program_id(1)

    @pl.when(k == 0)
    def _():
        o_ref[...] = p

    @pl.when(k > 0)
    def _():
        o_ref[...] += p


def _matmul_k(a, b, name="matmul_k"):
    m, kdim = a.shape
    n = b.shape[1]
    tm = _pick(m, (512, 256, 128))
    tk = _pick(kdim, (1024, 512, 256, 128))
    return pl.pallas_call(
        _mmk_kernel, grid=(m // tm, kdim // tk),
        in_specs=[pl.BlockSpec((tm, tk), lambda i, k: (i, k)),
                  pl.BlockSpec((tk, n), lambda i, k: (k, 0))],
        out_specs=pl.BlockSpec((tm, n), lambda i, k: (i, 0)),
        out_shape=jax.ShapeDtypeStruct((m, n), F32),
        compiler_params=_params(("arbitrary", "arbitrary"), (tm * tk + tk * n) * 2 + tm * n * 4, tm * n * 4),
        name=name,
    )(a, b)


def _ab_small_kernel(h_ref, w_ref, gq_ref, gkv_ref, *rest, rope):
    z = jnp.dot(h_ref[...], w_ref[...], preferred_element_type=F32)
    if rope:
        cos_ref, s_up_ref, s_dn_ref, qn_ref, ckv_ref, kr_ref = rest
    else:
        qn_ref, ckv_ref, kr_ref = rest
    qn_ref[...] = _rms(z[:, :Q_LORA], gq_ref[...]).astype(BF16)
    ckv_ref[...] = _rms(z[:, Q_LORA:Q_LORA + KV_LORA], gkv_ref[...])
    kr = z[:, Q_LORA + KV_LORA:]
    if rope:
        kr = _rope(kr, cos_ref, s_up_ref, s_dn_ref, MLA_ROPE // 4)
    kr_ref[...] = kr


def _ab_small(h, w_small, g_q, g_kv, tables):
    t, d = h.shape
    n = w_small.shape[1]
    tm = _pick(t, (512, 256, 128))
    rope = tables is not None
    in_specs = [pl.BlockSpec((tm, d), lambda i: (i, 0)),
                pl.BlockSpec((d, n), lambda i: (0, 0)),
                pl.BlockSpec((1, Q_LORA), lambda i: (0, 0)),
                pl.BlockSpec((1, KV_LORA), lambda i: (0, 0))]
    args = [h, w_small, g_q.reshape(1, Q_LORA), g_kv.reshape(1, KV_LORA)]
    if rope:
        tm = min(tm, tables[0].shape[0])
        in_specs[0] = pl.BlockSpec((tm, d), lambda i: (i, 0))
        pos_blocks = tables[0].shape[0] // tm
        in_specs += [pl.BlockSpec((tm, LANES), lambda i: (i % pos_blocks, 0))] * 3
        args += list(tables)
    return pl.pallas_call(
        functools.partial(_ab_small_kernel, rope=rope), grid=(t // tm,),
        in_specs=in_specs,
        out_specs=[pl.BlockSpec((tm, Q_LORA), lambda i: (i, 0)),
                   pl.BlockSpec((tm, KV_LORA), lambda i: (i, 0)),
                   pl.BlockSpec((tm, LANES), lambda i: (i, 0))],
        out_shape=[jax.ShapeDtypeStruct((t, Q_LORA), BF16),
                   jax.ShapeDtypeStruct((t, KV_LORA), F32),
                   jax.ShapeDtypeStruct((t, LANES), F32)],
        compiler_params=_params(("arbitrary",), (tm * d + d * n) * 2 + tm * n * 4, 3 * tm * n * 4),
        name="ab_small_proj",
    )(*args)


def _mla_kv_kernel(ckv_ref, kr_ref, w_ref, k_ref, v_ref):
    kv = jnp.dot(ckv_ref[...], w_ref[...], preferred_element_type=F32)
    kr = kr_ref[...]
    width = MLA_HEADS * MLA_NOPE
    for h in range(MLA_HEADS):
        k_ref[:, h * MXU_DEPTH:h * MXU_DEPTH + MLA_NOPE] = kv[:, h * MLA_NOPE:(h + 1) * MLA_NOPE].astype(BF16)
        k_ref[:, h * MXU_DEPTH + MLA_NOPE:(h + 1) * MXU_DEPTH] = kr
    v_ref[...] = kv[:, width:].astype(BF16)


def _mla_kv(ckv, kr, w_kv):
    t = ckv.shape[0]
    tm = _pick(t, (512, 256, 128))
    kw = MLA_HEADS * MXU_DEPTH
    vw = MLA_HEADS * MLA_V
    return pl.pallas_call(
        _mla_kv_kernel, grid=(t // tm,),
        in_specs=[pl.BlockSpec((tm, KV_LORA), lambda i: (i, 0)),
                  pl.BlockSpec((tm, LANES), lambda i: (i, 0)),
                  pl.BlockSpec(w_kv.shape, lambda i: (0, 0))],
        out_specs=[pl.BlockSpec((tm, kw), lambda i: (i, 0)),
                   pl.BlockSpec((tm, vw), lambda i: (i, 0))],
        out_shape=[jax.ShapeDtypeStruct((t, kw), BF16), jax.ShapeDtypeStruct((t, vw), BF16)],
        compiler_params=_params(("arbitrary",), (tm * (KV_LORA + LANES + kw + vw) + w_kv.size) * 2,
                                2 * tm * w_kv.shape[1] * 4),
        name="mla_kv_up",
    )(ckv, kr, w_kv)


def _dot_nt(a, b):
    return lax.dot_general(a, b, (((1,), (1,)), ((), ())), preferred_element_type=F32)


def _mla_attn_kernel(q_ref, k_ref, v_ref, o_ref):
    s = _dot_nt(q_ref[...], k_ref[...]) * MLA_SCALE
    p = jnp.exp(s - jnp.max(s, axis=-1, keepdims=True))
    l = jnp.sum(p, axis=-1, keepdims=True)
    o = jnp.dot(p.astype(BF16), v_ref[...], preferred_element_type=F32)
    o_ref[...] = (o / l).astype(BF16)


def _mla_attention(q_cat, k_cat, v, batch):
    nq = q_cat.shape[0] // batch
    nk = k_cat.shape[0] // batch
    tq = _pick(nq, (256, 128))
    qb = nq // tq
    return pl.pallas_call(
        _mla_attn_kernel, grid=(batch, MLA_HEADS, qb),
        in_specs=[pl.BlockSpec((tq, MXU_DEPTH), lambda b, h, i: (b * qb + i, h)),
                  pl.BlockSpec((nk, MXU_DEPTH), lambda b, h, i: (b, h)),
                  pl.BlockSpec((nk, MLA_V), lambda b, h, i: (b, h))],
        out_specs=pl.BlockSpec((tq, MLA_V), lambda b, h, i: (b * qb + i, h)),
        out_shape=jax.ShapeDtypeStruct((batch * nq, MLA_HEADS * MLA_V), BF16),
        compiler_params=_params(("arbitrary", "arbitrary", "arbitrary"),
                                (tq * MXU_DEPTH + nk * MXU_DEPTH + nk * MLA_V + tq * MLA_V) * 2, 4 * tq * nk * 4),
        name="mla_attention",
    )(q_cat, k_cat, v)


def _sink_softmax_pv(scores, values, sink):
    m = sink
    for s in scores:
        m = jnp.maximum(m, jnp.max(s, axis=-1, keepdims=True))
    l = jnp.exp(sink - m)
    o = None
    for s, v in zip(scores, values):
        p = jnp.exp(s - m)
        l = l + jnp.sum(p, axis=-1, keepdims=True)
        pv = jnp.dot(p.astype(BF16), v, preferred_element_type=F32)
        o = pv if o is None else o + pv
    return o / l


def _swa_lat_kernel(sink_ref, q_ref, k_ref, v_ref, kc_ref, vc_ref, o_ref, *, tq, win, n):
    kvh = pl.program_id(1)
    st = pl.program_id(2) * tq
    ws = pl.multiple_of(jnp.clip(st - WINDOW, 0, n - win), LANES)
    kw = k_ref[pl.ds(ws, win), :]
    vw = v_ref[pl.ds(ws, win), :]
    kc = kc_ref[...].astype(BF16)
    vc = vc_ref[...].astype(BF16)
    qpos = st + lax.broadcasted_iota(jnp.int32, (tq, win), 0)
    kpos = ws + lax.broadcasted_iota(jnp.int32, (tq, win), 1)
    band = jnp.abs(qpos - kpos) <= WINDOW
    neg = jnp.finfo(F32).min
    for g in range(SWA_GROUP):
        q = q_ref[:, g * HEAD_DIM:(g + 1) * HEAD_DIM]
        s_loc = jnp.where(band, _dot_nt(q, kw) * SWA_SCALE, neg)
        s_ctx = _dot_nt(q, kc) * SWA_SCALE
        sink = jnp.full((tq, 1), sink_ref[kvh * SWA_GROUP + g], F32)
        o = _sink_softmax_pv([s_loc, s_ctx], [vw, vc], sink)
        o_ref[:, g * HEAD_DIM:(g + 1) * HEAD_DIM] = o.astype(BF16)


def _swa_latent(z, k_ctx, v_ctx, sink, batch):
    n = z.shape[0] // batch
    past = k_ctx.shape[1]
    tq = _pick(n, (256, 128))
    win = min(n, tq + 2 * WINDOW)
    qb = n // tq
    qw = SWA_GROUP * HEAD_DIM
    k_col = C_OUT // HEAD_DIM
    v_col = (C_OUT + C_KV) // HEAD_DIM
    kern = functools.partial(_swa_lat_kernel, tq=tq, win=win, n=n)
    return pl.pallas_call(
        kern, grid=(batch, SWA_KV_HEADS, qb),
        in_specs=[pl.BlockSpec(memory_space=pltpu.SMEM),
                  pl.BlockSpec((tq, qw), lambda b, h, i: (b * qb + i, h)),
                  pl.BlockSpec((n, HEAD_DIM), lambda b, h, i: (b, k_col + h)),
                  pl.BlockSpec((n, HEAD_DIM), lambda b, h, i: (b, v_col + h)),
                  pl.BlockSpec((None, past, HEAD_DIM), lambda b, h, i: (b, 0, h)),
                  pl.BlockSpec((None, past, HEAD_DIM), lambda b, h, i: (b, 0, h))],
        out_specs=pl.BlockSpec((tq, qw), lambda b, h, i: (b * qb + i, h)),
        out_shape=jax.ShapeDtypeStruct((batch * n, C_OUT), BF16),
        compiler_params=_params(("arbitrary", "arbitrary", "arbitrary"),
                                (2 * tq * qw + 2 * n * HEAD_DIM) * 2 + 2 * past * HEAD_DIM * 4,
                                6 * tq * (win + past) * 4),
        name="swa_latent_attention",
    )(sink, z, z, z, k_ctx, v_ctx)


def _swa_ctx_kernel(sink_ref, q_ref, k_ref, v_ref, o_ref):
    kvh = pl.program_id(1)
    k = k_ref[...].astype(BF16)
    v = v_ref[...].astype(BF16)
    tq = q_ref.shape[0]
    for g in range(SWA_GROUP):
        q = q_ref[:, g * HEAD_DIM:(g + 1) * HEAD_DIM].astype(BF16)
        s = _dot_nt(q, k) * SWA_SCALE
        sink = jnp.full((tq, 1), sink_ref[kvh * SWA_GROUP + g], F32)
        o_ref[:, g * HEAD_DIM:(g + 1) * HEAD_DIM] = _sink_softmax_pv([s], [v], sink).astype(BF16)


def _swa_context(z, sink, batch):
    n = z.shape[0] // batch
    qw = SWA_GROUP * HEAD_DIM
    k_col = C_OUT // HEAD_DIM
    v_col = (C_OUT + C_KV) // HEAD_DIM
    return pl.pallas_call(
        _swa_ctx_kernel, grid=(batch, SWA_KV_HEADS),
        in_specs=[pl.BlockSpec(memory_space=pltpu.SMEM),
                  pl.BlockSpec((n, qw), lambda b, h: (b, h)),
                  pl.BlockSpec((n, HEAD_DIM), lambda b, h: (b, k_col + h)),
                  pl.BlockSpec((n, HEAD_DIM), lambda b, h: (b, v_col + h))],
        out_specs=pl.BlockSpec((n, qw), lambda b, h: (b, h)),
        out_shape=jax.ShapeDtypeStruct((batch * n, C_OUT), BF16),
        compiler_params=_params(("arbitrary", "arbitrary"), n * qw * 6 + 2 * n * HEAD_DIM * 4, 6 * n * n * 4),
        name="swa_context_attention",
    )(sink, z, z, z)


def _gate_kernel(uv_ref, ws_ref, bs_ref, a_ref, *, chunks):
    for c in range(chunks):
        rows = slice(c * CHUNK, (c + 1) * CHUNK)
        v = uv_ref[rows, A_WIDTH:]
        mu = jnp.mean(v, axis=-1, keepdims=True)
        vc = v - mu
        vn = (vc * lax.rsqrt(jnp.mean(vc * vc, axis=-1, keepdims=True) + EPS)).astype(BF16)
        for g in range(A_GROUPS):
            cols = slice(g * A_DIM, (g + 1) * A_DIM)
            mixed = jnp.dot(ws_ref[g], vn[:, cols], preferred_element_type=F32) + bs_ref[:, g:g + 1]
            a_ref[rows, cols] = (uv_ref[rows, cols] * mixed).astype(BF16)


def _spatial_gate(uv, w_s, b_s_t):
    t = uv.shape[0]
    tm = _pick(t, (256, 128))
    return pl.pallas_call(
        functools.partial(_gate_kernel, chunks=tm // CHUNK), grid=(t // tm,),
        in_specs=[pl.BlockSpec((tm, 2 * A_WIDTH), lambda i: (i, 0)),
                  pl.BlockSpec(w_s.shape, lambda i: (0, 0, 0)),
                  pl.BlockSpec(b_s_t.shape, lambda i: (0, 0))],
        out_specs=pl.BlockSpec((tm, A_WIDTH), lambda i: (i, 0)),
        out_shape=jax.ShapeDtypeStruct((t, A_WIDTH), BF16),
        compiler_params=_params(("arbitrary",), tm * 2 * A_WIDTH * 4 + tm * A_WIDTH * 2 + w_s.size * 2,
                                4 * CHUNK * A_WIDTH * 4),
        name="spatial_gate",
    )(uv, w_s, b_s_t)


def _rope_tables(n, rot_dim):
    rows = n // GRID_W
    pos_r = jnp.repeat(jnp.arange(rows, dtype=F32), GRID_W)
    pos_c = jnp.tile(jnp.arange(GRID_W, dtype=F32), rows)
    half = rot_dim // 2
    quarter = half // 2
    inv_freq = 1.0 / (ROPE_BASE ** (jnp.arange(0, half, 2, dtype=F32) / half))
    ang_r = pos_r[:, None] * inv_freq[None, :]
    ang_c = pos_c[:, None] * inv_freq[None, :]
    cos = jnp.concatenate([jnp.cos(ang_r), jnp.cos(ang_r), jnp.cos(ang_c), jnp.cos(ang_c)], axis=-1)
    sin = jnp.concatenate([jnp.sin(ang_r), jnp.sin(ang_r), jnp.sin(ang_c), jnp.sin(ang_c)], axis=-1)
    first = (jnp.arange(rot_dim) % half) < quarter
    s_up = jnp.where(first[None, :], -sin, 0.0)
    s_dn = jnp.where(first[None, :], 0.0, sin)
    pad = LANES - rot_dim
    if pad:
        cos = jnp.pad(cos, ((0, 0), (0, pad)), constant_values=1.0)
        s_up = jnp.pad(s_up, ((0, 0), (0, pad)))
        s_dn = jnp.pad(s_dn, ((0, 0), (0, pad)))
    return cos, s_up, s_dn


def _ffn_weights(w_gate, w_up, w_down):
    d, ff = w_gate.shape
    ffp = -(-ff // FF_PAD) * FF_PAD
    g = jnp.pad(w_gate.astype(BF16), ((0, 0), (0, ffp - ff))).reshape(d, ffp // HALF_GU, 1, HALF_GU)
    u = jnp.pad(w_up.astype(BF16), ((0, 0), (0, ffp - ff))).reshape(d, ffp // HALF_GU, 1, HALF_GU)
    w_gu = jnp.concatenate([g, u], axis=2).reshape(d, 2 * ffp)
    w_d = jnp.pad(w_down.astype(BF16), ((0, ffp - ff), (0, 0)))
    return w_gu, w_d


def _ab_weights(w_in, w_uq, w_ukv, w_s, b_s, w_out):
    o1, o2 = A_WIDTH, 2 * A_WIDTH
    w_uv = w_in[:, :o2].astype(BF16)
    w_small = jnp.pad(w_in[:, o2:], ((0, 0), (0, LANES - MLA_ROPE))).astype(BF16)
    uq = w_uq.reshape(Q_LORA, MLA_HEADS, MLA_NOPE + MLA_ROPE)
    uq = jnp.pad(uq, ((0, 0), (0, 0), (0, MXU_DEPTH - MLA_NOPE - MLA_ROPE)))
    w_q = uq.reshape(Q_LORA, MLA_HEADS * MXU_DEPTH).astype(BF16)
    ukv = w_ukv.reshape(KV_LORA, MLA_HEADS, MLA_NOPE + MLA_V)
    w_kv = jnp.concatenate([ukv[:, :, :MLA_NOPE].reshape(KV_LORA, -1),
                            ukv[:, :, MLA_NOPE:].reshape(KV_LORA, -1)], axis=1).astype(BF16)
    return dict(w_uv=w_uv, w_small=w_small, w_q=w_q, w_kv=w_kv, w_s=w_s.astype(BF16),
                b_s_t=jnp.transpose(b_s), w_out_a=w_out[:o1].astype(BF16), w_out_o=w_out[o1:].astype(BF16))


def _mixer_ab(h, w, g_q, g_kv, batch, tables, ckv_ctx=None, kr_ctx=None):
    t = h.shape[0]
    n = t // batch
    uv = _matmul([h], [w["w_uv"]], F32, epilogue="gelu", name="ab_uv_proj")
    a = _spatial_gate(uv, w["w_s"], w["b_s_t"])
    qn, ckv, kr = _ab_small(h, w["w_small"], g_q, g_kv, tables)
    if tables is None:
        q_cat = _matmul([qn], [w["w_q"]], BF16, name="mla_q_proj")
    else:
        q_cat = _matmul([qn], [w["w_q"]], BF16, epilogue="rope", tables=tables, rope_shift=MLA_ROPE // 4,
                        rope_groups="odd", name="mla_q_proj")
    ckv_all = ckv.astype(BF16).reshape(batch, n, KV_LORA)
    kr_all = kr.astype(BF16).reshape(batch, n, LANES)
    if ckv_ctx is not None:
        kr_pad = jnp.pad(kr_ctx.astype(BF16), ((0, 0), (0, 0), (0, LANES - MLA_ROPE)))
        ckv_all = jnp.concatenate([ckv_all, ckv_ctx.astype(BF16)], axis=1)
        kr_all = jnp.concatenate([kr_all, kr_pad], axis=1)
    nk = ckv_all.shape[1]
    k_cat, v = _mla_kv(ckv_all.reshape(batch * nk, KV_LORA), kr_all.reshape(batch * nk, LANES), w["w_kv"])
    o = _mla_attention(q_cat, k_cat, v, batch)
    y = _matmul([a, o], [w["w_out_a"], w["w_out_o"]], F32, name="ab_out_proj")
    return y, ckv, kr


def _mixer_c(h, w_in, sink, w_out, batch, tables, k_ctx=None, v_ctx=None):
    if tables is None:
        z = _matmul([h], [w_in], F32, name="swa_in_proj")
        o = _swa_context(z, sink, batch)
    else:
        z = _matmul([h], [w_in], BF16, epilogue="rope", tables=tables, rope_shift=HEAD_DIM // 4,
                    rope_cols=C_OUT + C_KV, name="swa_in_proj")
        o = _swa_latent(z, k_ctx, v_ctx, sink, batch)
    return _matmul([o], [w_out], F32, name="swa_out_proj"), z


def _trunk(x, batch, mods, row0, rows_per_mod, prm, caches):
    depth = mods.shape[0]
    n = x.shape[0] // batch
    latent = caches is not None
    tab_mla = _rope_tables(n, MLA_ROPE) if latent else None
    tab_swa = _rope_tables(n, HEAD_DIM) if latent else None
    new = dict(ckv=[], kr=[], k=[], v=[])
    post = functools.partial(_post, rows_per_mod=rows_per_mod, row0=row0)
    h = _pre(x, mods[0], prm["norm_pre"][0, 0], 0, rows_per_mod, row0)
    for l in range(depth):
        j = l // 2
        m = mods[l]
        w_gu, w_d = prm["ffn"][l][0]
        y = _matmul_k(_matmul([h], [w_gu], BF16, epilogue="swiglu", name="ffn_gate_up"), w_d, name="ffn_down")
        x, h = post(x, y, m, prm["norm_post"][l, 0], 2, MACARON, mods_next=m, g_next=prm["norm_pre"][l, 1],
                    base_next=3)
        if l % 2 == 0:
            ckv_ctx = caches["mla_ckv"][:, j] if latent else None
            kr_ctx = caches["mla_krope"][:, j] if latent else None
            y, ckv, kr = _mixer_ab(h, prm["ab"][j], prm["g_q"][j], prm["g_kv"][j], batch, tab_mla, ckv_ctx, kr_ctx)
            new["ckv"].append(ckv.reshape(batch, n, KV_LORA))
            new["kr"].append(kr[:, :MLA_ROPE].reshape(batch, n, MLA_ROPE))
        else:
            past_k = caches["swa_k"][:, j].reshape(batch, -1, C_KV) if latent else None
            past_v = caches["swa_v"][:, j].reshape(batch, -1, C_KV) if latent else None
            y, z = _mixer_c(h, prm["w_in_c"][j], prm["sink_c"][j], prm["w_out_c"][j], batch, tab_swa, past_k, past_v)
            new["k"].append(z[:, C_OUT:C_OUT + C_KV].reshape(batch, n, SWA_KV_HEADS, HEAD_DIM))
            new["v"].append(z[:, C_OUT + C_KV:].reshape(batch, n, SWA_KV_HEADS, HEAD_DIM))
        x, h = post(x, y, m, prm["norm_post"][l, 1], 5, 1.0, mods_next=m, g_next=prm["norm_pre"][l, 2], base_next=6)
        w_gu, w_d = prm["ffn"][l][1]
        y = _matmul_k(_matmul([h], [w_gu], BF16, epilogue="swiglu", name="ffn_gate_up"), w_d, name="ffn_down")
        if l + 1 < depth:
            x, h = post(x, y, m, prm["norm_post"][l, 2], 8, MACARON, mods_next=mods[l + 1],
                        g_next=prm["norm_pre"][l + 1, 0], base_next=0)
        else:
            x, _ = post(x, y, m, prm["norm_post"][l, 2], 8, MACARON)
    return x, new


def kernel(x_prompt, x_sample, cache_mla_ckv, cache_mla_krope, cache_swa_k, cache_swa_v, c, c_ctx, w_mod, b_mod,
           norm_pre, norm_post, w_ffn_gate, w_ffn_up, w_ffn_down, w_in_ab, g_q, g_kv, w_uq, w_ukv, w_s, b_s,
           w_out_ab, w_in_c, sink_c, w_out_c):
    depth = w_mod.shape[0]
    batch, seq, d = x_prompt.shape
    dec_batch, dec_seq, _ = x_sample.shape
    assert 1 + dec_batch <= MOD_ROWS

    cond = jnp.concatenate([c_ctx[None, :], c], axis=0)
    cond = jnp.pad(cond, ((0, MOD_ROWS - cond.shape[0]), (0, 0)))
    mods = _ada_mods(cond, w_mod, b_mod)

    prm = dict(
        norm_pre=norm_pre, norm_post=norm_post, g_q=g_q, g_kv=g_kv, sink_c=sink_c,
        ffn=[[_ffn_weights(w_ffn_gate[l, s], w_ffn_up[l, s], w_ffn_down[l, s]) for s in range(2)]
             for l in range(depth)],
        ab=[_ab_weights(w_in_ab[j], w_uq[j], w_ukv[j], w_s[j], b_s[j], w_out_ab[j]) for j in range(w_in_ab.shape[0])],
        w_in_c=[w_in_c[j].astype(BF16) for j in range(w_in_c.shape[0])],
        w_out_c=[w_out_c[j].astype(BF16) for j in range(w_out_c.shape[0])],
    )

    y_prompt, new = _trunk(x_prompt.reshape(batch * seq, d), batch, mods, 0, batch * seq, prm, None)
    caches = dict(mla_ckv=cache_mla_ckv, mla_krope=cache_mla_krope, swa_k=cache_swa_k, swa_v=cache_swa_v)
    y_sample, _ = _trunk(x_sample.reshape(dec_batch * dec_seq, d), dec_batch, mods, 1, dec_seq, prm, caches)

    return (y_prompt.reshape(batch, seq, d), y_sample.reshape(dec_batch, dec_seq, d),
            jnp.stack(new["ckv"], axis=1), jnp.stack(new["kr"], axis=1),
            jnp.stack(new["k"], axis=1), jnp.stack(new["v"], axis=1))
```

```python
import functools

import jax
import jax.numpy as jnp
from jax import lax
from jax.experimental import pallas as pl
from jax.experimental.pallas import tpu as pltpu

F32 = jnp.float32
BF16 = jnp.bfloat16

GRID_W = 64
CHUNK = 128
A_GROUPS = 16
A_DIM = 128
A_WIDTH = A_GROUPS * A_DIM
MLA_HEADS = 16
MLA_NOPE = 128
MLA_ROPE = 64
MLA_V = 128
Q_LORA = 1024
KV_LORA = 512
MLA_SCALE = (MLA_NOPE + MLA_ROPE) ** -0.5
SWA_HEADS = 32
SWA_KV_HEADS = 8
SWA_GROUP = SWA_HEADS // SWA_KV_HEADS
HEAD_DIM = 128
WINDOW = 128
SWA_SCALE = HEAD_DIM ** -0.5
MACARON = 0.5
ROPE_BASE = 10000.0
LOG2E = 1.4426950408889634
EPS = 1e-6
N_MOD = 9
C_KV = SWA_KV_HEADS * HEAD_DIM
C_OUT = SWA_HEADS * HEAD_DIM

LANES = 128
MXU_DEPTH = 256
V7X_VMEM_BYTES = 64 * 1024 * 1024
VMEM_REQUEST_CAP = V7X_VMEM_BYTES - 6 * 1024 * 1024
MOD_ROWS = 16
FF_PAD = 1024
FFN_TN = 512
FFN_DOWN_TM = 512
MLA_TQ = 1024
MLA_KEY_CHUNK = 1088


def _pick(n, prefs):
    for p in prefs:
        if p <= n and n % p == 0:
            return p
    return n


def _params(semantics, block_bytes, temp_bytes=0):
    need = 2 * block_bytes + temp_bytes + (2 << 20)
    return pltpu.CompilerParams(dimension_semantics=semantics,
                                vmem_limit_bytes=int(min(max(need, 16 << 20), VMEM_REQUEST_CAP)))


def _rms(x, g):
    return x * lax.rsqrt(jnp.mean(x * x, axis=-1, keepdims=True) + EPS) * g


def _mod_kernel(c_ref, w_ref, b_ref, o_ref):
    c = c_ref[...]
    s = (c * jax.nn.sigmoid(c)).astype(BF16)
    o_ref[...] = jnp.dot(s, w_ref[...].astype(BF16), preferred_element_type=F32) + b_ref[...]


def _ada_mods(cond, w_mod, b_mod):
    depth, d, n = w_mod.shape
    tn = _pick(n, (1024, 512, 256, 128))
    out = pl.pallas_call(
        _mod_kernel,
        grid=(depth, n // tn),
        in_specs=[pl.BlockSpec((MOD_ROWS, d), lambda l, j: (0, 0)),
                  pl.BlockSpec((None, d, tn), lambda l, j: (l, 0, j)),
                  pl.BlockSpec((None, 1, tn), lambda l, j: (l, 0, j))],
        out_specs=pl.BlockSpec((None, MOD_ROWS, tn), lambda l, j: (l, 0, j)),
        out_shape=jax.ShapeDtypeStruct((depth, MOD_ROWS, n), F32),
        compiler_params=_params(("arbitrary", "arbitrary"), MOD_ROWS * d * 4 + d * tn * 4 + MOD_ROWS * tn * 4,
                                d * tn * 2),
        name="ada_mods",
    )(cond, w_mod, b_mod.reshape(depth, 1, n))
    return out.reshape(depth, MOD_ROWS, N_MOD, d)


def _pre_kernel(x_ref, m_ref, g_ref, h_ref, *, base):
    n = _rms(x_ref[...], g_ref[...])
    h_ref[...] = (n * (1.0 + m_ref[base + 1:base + 2, :]) + m_ref[base:base + 1, :]).astype(BF16)


def _post_kernel(x_ref, y_ref, mc_ref, gp_ref, *rest, gate, weight, base_next):
    x = x_ref[...] + weight * mc_ref[gate:gate + 1, :] * _rms(y_ref[...], gp_ref[...])
    if base_next is None:
        (xo_ref,) = rest
    else:
        mn_ref, gn_ref, xo_ref, h_ref = rest
        n = _rms(x, gn_ref[...])
        h_ref[...] = (n * (1.0 + mn_ref[base_next + 1:base_next + 2, :])
                      + mn_ref[base_next:base_next + 1, :]).astype(BF16)
    xo_ref[...] = x


def _row_specs(t, d, tm, rows_per_mod, row0):
    row = pl.BlockSpec((tm, d), lambda i: (i, 0))
    mod = pl.BlockSpec((None, N_MOD, d), lambda i: (row0 + (i * tm) // rows_per_mod, 0, 0))
    gain = pl.BlockSpec((1, d), lambda i: (0, 0))
    return row, mod, gain


def _pre(x, mods, g_pre, base, rows_per_mod, row0):
    t, d = x.shape
    tm = _pick(min(t, rows_per_mod), (256, 128))
    row, mod, gain = _row_specs(t, d, tm, rows_per_mod, row0)
    return pl.pallas_call(
        functools.partial(_pre_kernel, base=base),
        grid=(t // tm,),
        in_specs=[row, mod, gain],
        out_specs=row,
        out_shape=jax.ShapeDtypeStruct((t, d), BF16),
        compiler_params=_params(("arbitrary",), tm * d * 6 + N_MOD * d * 8, 3 * tm * d * 4),
        name="pre_norm",
    )(x, mods, g_pre.reshape(1, d))


def _post(x, y, mods_cur, g_post, gate, weight, rows_per_mod, row0, mods_next=None, g_next=None, base_next=None):
    t, d = x.shape
    tm = _pick(min(t, rows_per_mod), (256, 128))
    row, mod, gain = _row_specs(t, d, tm, rows_per_mod, row0)
    kern = functools.partial(_post_kernel, gate=gate, weight=weight, base_next=base_next)
    if base_next is None:
        return pl.pallas_call(
            kern, grid=(t // tm,),
            in_specs=[row, row, mod, gain],
            out_specs=row,
            out_shape=jax.ShapeDtypeStruct((t, d), F32),
            compiler_params=_params(("arbitrary",), 3 * tm * d * 4 + N_MOD * d * 4, 3 * tm * d * 4),
            name="post_norm",
        )(x, y, mods_cur, g_post.reshape(1, d)), None
    return pl.pallas_call(
        kern, grid=(t // tm,),
        in_specs=[row, row, mod, gain, mod, gain],
        out_specs=[row, row],
        out_shape=[jax.ShapeDtypeStruct((t, d), F32), jax.ShapeDtypeStruct((t, d), BF16)],
        compiler_params=_params(("arbitrary",), 3 * tm * d * 4 + tm * d * 2 + 2 * N_MOD * d * 4, 4 * tm * d * 4),
        name="post_pre_norm",
    )(x, y, mods_cur, g_post.reshape(1, d), mods_next, g_next.reshape(1, d))


def _rope(x, cos_ref, s_up_ref, s_dn_ref, shift):
    return (x * cos_ref[...] + pltpu.roll(x, LANES - shift, 1) * s_up_ref[...]
            + pltpu.roll(x, shift, 1) * s_dn_ref[...])


def _mm_kernel(*refs, n_a, n_b, epilogue, rope_shift, rope_groups, scale, scale_blocks):
    a_refs, b_refs = refs[:n_a], refs[n_a:n_a + n_b]
    o_ref = refs[-1]
    if epilogue == "swiglu":
        g = jnp.dot(a_refs[0][...], b_refs[0][...], preferred_element_type=F32)
        u = jnp.dot(a_refs[0][...], b_refs[1][...], preferred_element_type=F32)
        o_ref[...] = (g * jax.nn.sigmoid(g) * u).astype(o_ref.dtype)
        return
    acc = jnp.dot(a_refs[0][...], b_refs[0][...], preferred_element_type=F32)
    for a_ref, b_ref in zip(a_refs[1:], b_refs[1:]):
        acc = acc + jnp.dot(a_ref[...], b_ref[...], preferred_element_type=F32)
    if scale is not None:
        if scale_blocks is None:
            acc = acc * scale
        else:
            acc = acc * jnp.where(pl.program_id(1) < scale_blocks, scale, 1.0).astype(F32)
    if epilogue == "gelu":
        o_ref[...] = jax.nn.gelu(acc, approximate=True).astype(o_ref.dtype)
    elif epilogue == "rope":
        cos_ref, s_up_ref, s_dn_ref = refs[n_a + n_b:n_a + n_b + 3]
        for c in range(acc.shape[1] // LANES):
            x = acc[:, c * LANES:(c + 1) * LANES]
            if rope_groups == "all" or c % 2 == 1:
                x = _rope(x, cos_ref, s_up_ref, s_dn_ref, rope_shift)
            o_ref[:, c * LANES:(c + 1) * LANES] = x.astype(o_ref.dtype)
    else:
        o_ref[...] = acc.astype(o_ref.dtype)


def _matmul(a_list, b_list, out_dtype, epilogue=None, tables=None, rope_shift=None, rope_groups="all",
            rope_cols=None, scale=None, scale_cols=None, tm=None, tn=None, name="matmul"):
    m = a_list[0].shape[0]
    n = (b_list[0][0] if isinstance(b_list[0], tuple) else b_list[0]).shape[-1]
    tm = _pick(m, (tm or 1024, 512, 256, 128))
    tn = _pick(n, (tn or 1024, 512, 256, 128))
    in_specs = [pl.BlockSpec((tm, a.shape[1]), lambda i, j: (i, 0)) for a in a_list]
    args = list(a_list)
    k_total = 0
    for b in b_list:
        if isinstance(b, tuple):
            arr, idx = b
            in_specs.append(pl.BlockSpec((None, arr.shape[1], tn), lambda i, j, idx=idx: (idx, 0, j)))
        else:
            arr = b
            in_specs.append(pl.BlockSpec((arr.shape[0], tn), lambda i, j: (0, j)))
        args.append(arr)
        k_total += arr.shape[-2]
    block_bytes = (tm * sum(a.shape[1] for a in a_list) + k_total * tn) * 2 + tm * tn * jnp.dtype(out_dtype).itemsize
    if epilogue == "rope":
        n_pos = tables[0].shape[0] // 2
        tm = min(tm, n_pos)
        in_specs[:len(a_list)] = [pl.BlockSpec((tm, a.shape[1]), lambda i, j: (i, 0)) for a in a_list]
        pos_blocks = n_pos // tm
        rope_blocks = (n if rope_cols is None else rope_cols) // tn
        assert rope_blocks * tn == (n if rope_cols is None else rope_cols)
        in_specs += [pl.BlockSpec(
            (tm, LANES), lambda i, j: (i % pos_blocks + jnp.where(j < rope_blocks, 0, pos_blocks), 0))] * 3
        args += list(tables)
        block_bytes += 3 * tm * LANES * 4
    scale_blocks = None
    if scale_cols is not None and scale_cols < n:
        assert scale_cols % tn == 0
        scale_blocks = scale_cols // tn
    kern = functools.partial(_mm_kernel, n_a=len(a_list), n_b=len(b_list), epilogue=epilogue, rope_shift=rope_shift,
                             rope_groups=rope_groups, scale=scale, scale_blocks=scale_blocks)
    return pl.pallas_call(
        kern, grid=(m // tm, n // tn),
        in_specs=in_specs,
        out_specs=pl.BlockSpec((tm, tn), lambda i, j: (i, j)),
        out_shape=jax.ShapeDtypeStruct((m, n), out_dtype),
        compiler_params=_params(("arbitrary", "arbitrary"), block_bytes, 4 * len(b_list) * tm * tn * 4),
        name=name,
    )(*args)


def _cast_cols_kernel(x_ref, o_ref):
    n = x_ref.shape[1]
    o_ref[:, :n] = x_ref[...].astype(BF16)
    if o_ref.shape[1] > n:
        o_ref[:, n:] = jnp.zeros((o_ref.shape[0], o_ref.shape[1] - n), BF16)


def _cast_pad_cols(w, n_pad):
    layers, r, n = w.shape
    tr = _pick(r, (128,))
    return pl.pallas_call(
        _cast_cols_kernel, grid=(layers, r // tr),
        in_specs=[pl.BlockSpec((None, tr, n), lambda l, i: (l, i, 0))],
        out_specs=pl.BlockSpec((None, tr, n_pad), lambda l, i: (l, i, 0)),
        out_shape=jax.ShapeDtypeStruct((layers, r, n_pad), BF16),
        compiler_params=_params(("arbitrary", "arbitrary"), tr * n * 4 + tr * n_pad * 2, tr * n_pad * 4),
        name="cast_pad_cols",
    )(w)


def _cast_rows_kernel(x_ref, o_ref, *, valid_blocks):
    r = pl.program_id(1)

    @pl.when(r < valid_blocks)
    def _():
        o_ref[...] = x_ref[...].astype(BF16)

    @pl.when(r >= valid_blocks)
    def _():
        o_ref[...] = jnp.zeros(o_ref.shape, BF16)


def _cast_pad_rows(w, r_pad):
    layers, r, n = w.shape
    tr = _pick(r, (256, 128))
    assert r % tr == 0 and r_pad % tr == 0
    valid = r // tr
    return pl.pallas_call(
        functools.partial(_cast_rows_kernel, valid_blocks=valid), grid=(layers, r_pad // tr),
        in_specs=[pl.BlockSpec((None, tr, n), lambda l, i: (l, jnp.minimum(i, valid - 1), 0))],
        out_specs=pl.BlockSpec((None, tr, n), lambda l, i: (l, i, 0)),
        out_shape=jax.ShapeDtypeStruct((layers, r_pad, n), BF16),
        compiler_params=_params(("arbitrary", "arbitrary"), tr * n * 6, tr * n * 4),
        name="cast_pad_rows",
    )(w)


def _ab_small_kernel(h_ref, w_ref, gq_ref, gkv_ref, *rest, rope):
    z = jnp.dot(h_ref[...], w_ref[...], preferred_element_type=F32)
    if rope:
        cos_ref, s_up_ref, s_dn_ref, qn_ref, ckv_ref, kr_ref = rest
    else:
        qn_ref, ckv_ref, kr_ref = rest
    qn_ref[...] = _rms(z[:, :Q_LORA], gq_ref[...]).astype(BF16)
    ckv_ref[...] = _rms(z[:, Q_LORA:Q_LORA + KV_LORA], gkv_ref[...])
    kr = z[:, Q_LORA + KV_LORA:]
    if rope:
        kr = _rope(kr, cos_ref, s_up_ref, s_dn_ref, MLA_ROPE // 4)
    kr_ref[...] = kr


def _ab_small(h, w_small, g_q, g_kv, tables):
    t, d = h.shape
    n = w_small.shape[1]
    tm = _pick(t, (512, 256, 128))
    rope = tables is not None
    in_specs = [pl.BlockSpec((tm, d), lambda i: (i, 0)),
                pl.BlockSpec((d, n), lambda i: (0, 0)),
                pl.BlockSpec((1, Q_LORA), lambda i: (0, 0)),
                pl.BlockSpec((1, KV_LORA), lambda i: (0, 0))]
    args = [h, w_small, g_q.reshape(1, Q_LORA), g_kv.reshape(1, KV_LORA)]
    if rope:
        n_pos = tables[0].shape[0] // 2
        tm = min(tm, n_pos)
        in_specs[0] = pl.BlockSpec((tm, d), lambda i: (i, 0))
        pos_blocks = n_pos // tm
        in_specs += [pl.BlockSpec((tm, LANES), lambda i: (i % pos_blocks, 0))] * 3
        args += list(tables)
    return pl.pallas_call(
        functools.partial(_ab_small_kernel, rope=rope), grid=(t // tm,),
        in_specs=in_specs,
        out_specs=[pl.BlockSpec((tm, Q_LORA), lambda i: (i, 0)),
                   pl.BlockSpec((tm, KV_LORA), lambda i: (i, 0)),
                   pl.BlockSpec((tm, LANES), lambda i: (i, 0))],
        out_shape=[jax.ShapeDtypeStruct((t, Q_LORA), BF16),
                   jax.ShapeDtypeStruct((t, KV_LORA), F32),
                   jax.ShapeDtypeStruct((t, LANES), F32)],
        compiler_params=_params(("arbitrary",), (tm * d + d * n) * 2 + tm * n * 4, 3 * tm * n * 4),
        name="ab_small_proj",
    )(*args)


def _mla_kv_kernel(ckv_ref, kr_ref, w_ref, k_ref, v_ref):
    kv = jnp.dot(ckv_ref[...], w_ref[...], preferred_element_type=F32)
    kr = kr_ref[...]
    width = MLA_HEADS * MLA_NOPE
    for h in range(MLA_HEADS):
        k_ref[:, h * MXU_DEPTH:h * MXU_DEPTH + MLA_NOPE] = kv[:, h * MLA_NOPE:(h + 1) * MLA_NOPE].astype(BF16)
        k_ref[:, h * MXU_DEPTH + MLA_NOPE:(h + 1) * MXU_DEPTH] = kr
    v_ref[...] = kv[:, width:].astype(BF16)


def _mla_kv(ckv, kr, w_kv):
    t = ckv.shape[0]
    tm = _pick(t, (512, 256, 128))
    kw = MLA_HEADS * MXU_DEPTH
    vw = MLA_HEADS * MLA_V
    return pl.pallas_call(
        _mla_kv_kernel, grid=(t // tm,),
        in_specs=[pl.BlockSpec((tm, KV_LORA), lambda i: (i, 0)),
                  pl.BlockSpec((tm, LANES), lambda i: (i, 0)),
                  pl.BlockSpec(w_kv.shape, lambda i: (0, 0))],
        out_specs=[pl.BlockSpec((tm, kw), lambda i: (i, 0)),
                   pl.BlockSpec((tm, vw), lambda i: (i, 0))],
        out_shape=[jax.ShapeDtypeStruct((t, kw), BF16), jax.ShapeDtypeStruct((t, vw), BF16)],
        compiler_params=_params(("arbitrary",), (tm * (KV_LORA + LANES + kw + vw) + w_kv.size) * 2,
                                2 * tm * w_kv.shape[1] * 4),
        name="mla_kv_up",
    )(ckv, kr, w_kv)


def _dot_nt(a, b):
    return lax.dot_general(a, b, (((1,), (1,)), ((), ())), preferred_element_type=F32)


def _mla_attn_kernel(q_ref, k_ref, v_ref, o_ref, *, bounds):
    q = q_ref[...]
    m = l = o = None
    for lo, hi in bounds:
        s = _dot_nt(q, k_ref[lo:hi, :])
        m_c = jnp.max(s, axis=-1, keepdims=True)
        if m is None:
            m = m_c
            p = jnp.exp2(s - m)
            l = jnp.sum(p, axis=-1, keepdims=True)
            o = jnp.dot(p.astype(BF16), v_ref[lo:hi, :], preferred_element_type=F32)
        else:
            m_new = jnp.maximum(m, m_c)
            alpha = jnp.exp2(m - m_new)
            p = jnp.exp2(s - m_new)
            l = l * alpha + jnp.sum(p, axis=-1, keepdims=True)
            o = o * alpha + jnp.dot(p.astype(BF16), v_ref[lo:hi, :], preferred_element_type=F32)
            m = m_new
    o_ref[...] = (o / l).astype(BF16)


def _key_chunks(nk, target):
    groups = nk // LANES
    parts = max(1, round(nk / target))
    sizes = [groups // parts + (1 if c < groups % parts else 0) for c in range(parts)]
    edges = [0]
    for sz in sizes:
        edges.append(edges[-1] + sz * LANES)
    return tuple((edges[c], edges[c + 1]) for c in range(parts))


def _mla_attention(q_cat, k_cat, v, batch):
    nq = q_cat.shape[0] // batch
    nk = k_cat.shape[0] // batch
    tq = _pick(nq, (MLA_TQ, 128))
    qb = nq // tq
    return pl.pallas_call(
        functools.partial(_mla_attn_kernel, bounds=_key_chunks(nk, MLA_KEY_CHUNK)), grid=(batch, MLA_HEADS, qb),
        in_specs=[pl.BlockSpec((tq, MXU_DEPTH), lambda b, h, i: (b * qb + i, h)),
                  pl.BlockSpec((nk, MXU_DEPTH), lambda b, h, i: (b, h)),
                  pl.BlockSpec((nk, MLA_V), lambda b, h, i: (b, h))],
        out_specs=pl.BlockSpec((tq, MLA_V), lambda b, h, i: (b * qb + i, h)),
        out_shape=jax.ShapeDtypeStruct((batch * nq, MLA_HEADS * MLA_V), BF16),
        compiler_params=_params(("arbitrary", "arbitrary", "arbitrary"),
                                (tq * MXU_DEPTH + nk * MXU_DEPTH + nk * MLA_V + tq * MLA_V) * 2, 4 * tq * nk * 4),
        name="mla_attention",
    )(q_cat, k_cat, v)


def _sink_softmax_pv(scores, values, sink):
    m = sink
    for s in scores:
        m = jnp.maximum(m, jnp.max(s, axis=-1, keepdims=True))
    l = jnp.exp2(sink - m)
    o = None
    for s, v in zip(scores, values):
        p = jnp.exp2(s - m)
        l = l + jnp.sum(p, axis=-1, keepdims=True)
        pv = jnp.dot(p.astype(BF16), v, preferred_element_type=F32)
        o = pv if o is None else o + pv
    return o / l


def _swa_lat_kernel(sink_ref, q_ref, k_ref, v_ref, kc_ref, vc_ref, o_ref, *, tq, win, n):
    kvh = pl.program_id(1)
    st = pl.program_id(2) * tq
    ws = pl.multiple_of(jnp.clip(st - WINDOW, 0, n - win), LANES)
    kw = k_ref[pl.ds(ws, win), :]
    vw = v_ref[pl.ds(ws, win), :]
    kc = kc_ref[...].astype(BF16)
    vc = vc_ref[...].astype(BF16)
    qpos = st + lax.broadcasted_iota(jnp.int32, (tq, win), 0)
    kpos = ws + lax.broadcasted_iota(jnp.int32, (tq, win), 1)
    band = jnp.abs(qpos - kpos) <= WINDOW
    neg = jnp.finfo(F32).min
    for g in range(SWA_GROUP):
        q = q_ref[:, g * HEAD_DIM:(g + 1) * HEAD_DIM]
        s_loc = jnp.where(band, _dot_nt(q, kw), neg)
        s_ctx = _dot_nt(q, kc)
        sink = jnp.full((tq, 1), sink_ref[kvh * SWA_GROUP + g] * LOG2E, F32)
        o = _sink_softmax_pv([s_loc, s_ctx], [vw, vc], sink)
        o_ref[:, g * HEAD_DIM:(g + 1) * HEAD_DIM] = o.astype(BF16)


def _swa_latent(z, k_ctx, v_ctx, sink, batch):
    n = z.shape[0] // batch
    past = k_ctx.shape[1]
    tq = _pick(n, (256, 128))
    win = min(n, tq + 2 * WINDOW)
    qb = n // tq
    qw = SWA_GROUP * HEAD_DIM
    k_col = C_OUT // HEAD_DIM
    v_col = (C_OUT + C_KV) // HEAD_DIM
    kern = functools.partial(_swa_lat_kernel, tq=tq, win=win, n=n)
    return pl.pallas_call(
        kern, grid=(batch, SWA_KV_HEADS, qb),
        in_specs=[pl.BlockSpec(memory_space=pltpu.SMEM),
                  pl.BlockSpec((tq, qw), lambda b, h, i: (b * qb + i, h)),
                  pl.BlockSpec((n, HEAD_DIM), lambda b, h, i: (b, k_col + h)),
                  pl.BlockSpec((n, HEAD_DIM), lambda b, h, i: (b, v_col + h)),
                  pl.BlockSpec((None, past, HEAD_DIM), lambda b, h, i: (b, 0, h)),
                  pl.BlockSpec((None, past, HEAD_DIM), lambda b, h, i: (b, 0, h))],
        out_specs=pl.BlockSpec((tq, qw), lambda b, h, i: (b * qb + i, h)),
        out_shape=jax.ShapeDtypeStruct((batch * n, C_OUT), BF16),
        compiler_params=_params(("arbitrary", "arbitrary", "arbitrary"),
                                (2 * tq * qw + 2 * n * HEAD_DIM) * 2 + 2 * past * HEAD_DIM * 4,
                                6 * tq * (win + past) * 4),
        name="swa_latent_attention",
    )(sink, z, z, z, k_ctx, v_ctx)


def _swa_ctx_kernel(sink_ref, q_ref, k_ref, v_ref, o_ref):
    kvh = pl.program_id(1)
    k = k_ref[...].astype(BF16)
    v = v_ref[...].astype(BF16)
    tq = q_ref.shape[0]
    for g in range(SWA_GROUP):
        q = (q_ref[:, g * HEAD_DIM:(g + 1) * HEAD_DIM] * (SWA_SCALE * LOG2E)).astype(BF16)
        s = _dot_nt(q, k)
        sink = jnp.full((tq, 1), sink_ref[kvh * SWA_GROUP + g] * LOG2E, F32)
        o_ref[:, g * HEAD_DIM:(g + 1) * HEAD_DIM] = _sink_softmax_pv([s], [v], sink).astype(BF16)


def _swa_context(z, sink, batch):
    n = z.shape[0] // batch
    qw = SWA_GROUP * HEAD_DIM
    k_col = C_OUT // HEAD_DIM
    v_col = (C_OUT + C_KV) // HEAD_DIM
    return pl.pallas_call(
        _swa_ctx_kernel, grid=(batch, SWA_KV_HEADS),
        in_specs=[pl.BlockSpec(memory_space=pltpu.SMEM),
                  pl.BlockSpec((n, qw), lambda b, h: (b, h)),
                  pl.BlockSpec((n, HEAD_DIM), lambda b, h: (b, k_col + h)),
                  pl.BlockSpec((n, HEAD_DIM), lambda b, h: (b, v_col + h))],
        out_specs=pl.BlockSpec((n, qw), lambda b, h: (b, h)),
        out_shape=jax.ShapeDtypeStruct((batch * n, C_OUT), BF16),
        compiler_params=_params(("arbitrary", "arbitrary"), n * qw * 6 + 2 * n * HEAD_DIM * 4, 6 * n * n * 4),
        name="swa_context_attention",
    )(sink, z, z, z)


def _gate_kernel(uv_ref, ws_ref, bs_ref, a_ref, *, chunks):
    for c in range(chunks):
        rows = slice(c * CHUNK, (c + 1) * CHUNK)
        v = uv_ref[rows, A_WIDTH:]
        mu = jnp.mean(v, axis=-1, keepdims=True)
        vc = v - mu
        vn = (vc * lax.rsqrt(jnp.mean(vc * vc, axis=-1, keepdims=True) + EPS)).astype(BF16)
        for g in range(A_GROUPS):
            cols = slice(g * A_DIM, (g + 1) * A_DIM)
            mixed = jnp.dot(ws_ref[g], vn[:, cols], preferred_element_type=F32) + bs_ref[:, g:g + 1]
            a_ref[rows, cols] = (uv_ref[rows, cols] * mixed).astype(BF16)


def _spatial_gate(uv, w_s, b_s_t):
    t = uv.shape[0]
    tm = _pick(t, (256, 128))
    return pl.pallas_call(
        functools.partial(_gate_kernel, chunks=tm // CHUNK), grid=(t // tm,),
        in_specs=[pl.BlockSpec((tm, 2 * A_WIDTH), lambda i: (i, 0)),
                  pl.BlockSpec(w_s.shape, lambda i: (0, 0, 0)),
                  pl.BlockSpec(b_s_t.shape, lambda i: (0, 0))],
        out_specs=pl.BlockSpec((tm, A_WIDTH), lambda i: (i, 0)),
        out_shape=jax.ShapeDtypeStruct((t, A_WIDTH), BF16),
        compiler_params=_params(("arbitrary",), tm * 2 * A_WIDTH * 4 + tm * A_WIDTH * 2 + w_s.size * 2,
                                4 * CHUNK * A_WIDTH * 4),
        name="spatial_gate",
    )(uv, w_s, b_s_t)


def _rope_tables(n, rot_dim):
    rows = n // GRID_W
    pos_r = jnp.repeat(jnp.arange(rows, dtype=F32), GRID_W)
    pos_c = jnp.tile(jnp.arange(GRID_W, dtype=F32), rows)
    half = rot_dim // 2
    quarter = half // 2
    inv_freq = 1.0 / (ROPE_BASE ** (jnp.arange(0, half, 2, dtype=F32) / half))
    ang_r = pos_r[:, None] * inv_freq[None, :]
    ang_c = pos_c[:, None] * inv_freq[None, :]
    cos = jnp.concatenate([jnp.cos(ang_r), jnp.cos(ang_r), jnp.cos(ang_c), jnp.cos(ang_c)], axis=-1)
    sin = jnp.concatenate([jnp.sin(ang_r), jnp.sin(ang_r), jnp.sin(ang_c), jnp.sin(ang_c)], axis=-1)
    first = (jnp.arange(rot_dim) % half) < quarter
    s_up = jnp.where(first[None, :], -sin, 0.0)
    s_dn = jnp.where(first[None, :], 0.0, sin)
    pad = LANES - rot_dim
    if pad:
        cos = jnp.pad(cos, ((0, 0), (0, pad)), constant_values=1.0)
        s_up = jnp.pad(s_up, ((0, 0), (0, pad)))
        s_dn = jnp.pad(s_dn, ((0, 0), (0, pad)))
    zeros = jnp.zeros_like(cos)
    return (jnp.concatenate([cos, zeros + 1.0], axis=0), jnp.concatenate([s_up, zeros], axis=0),
            jnp.concatenate([s_dn, zeros], axis=0))


def _ffn(h, prm, idx):
    w_gate, w_up, w_down = prm["ffn"]
    a = _matmul([h], [(w_gate, idx), (w_up, idx)], BF16, epilogue="swiglu", tn=FFN_TN, name="ffn_gate_up")
    return _matmul([a], [(w_down, idx)], F32, tm=FFN_DOWN_TM, tn=FFN_TN, name="ffn_down")


def _ab_weights(w_in, w_uq, w_ukv, w_s, b_s, w_out):
    o1, o2 = A_WIDTH, 2 * A_WIDTH
    w_uv = w_in[:, :o2].astype(BF16)
    w_small = jnp.pad(w_in[:, o2:], ((0, 0), (0, LANES - MLA_ROPE))).astype(BF16)
    uq = w_uq.reshape(Q_LORA, MLA_HEADS, MLA_NOPE + MLA_ROPE)
    uq = jnp.pad(uq, ((0, 0), (0, 0), (0, MXU_DEPTH - MLA_NOPE - MLA_ROPE)))
    w_q = uq.reshape(Q_LORA, MLA_HEADS * MXU_DEPTH).astype(BF16)
    ukv = w_ukv.reshape(KV_LORA, MLA_HEADS, MLA_NOPE + MLA_V)
    w_kv = jnp.concatenate([ukv[:, :, :MLA_NOPE].reshape(KV_LORA, -1),
                            ukv[:, :, MLA_NOPE:].reshape(KV_LORA, -1)], axis=1).astype(BF16)
    return dict(w_uv=w_uv, w_small=w_small, w_q=w_q, w_kv=w_kv, w_s=w_s.astype(BF16),
                b_s_t=jnp.transpose(b_s), w_out_a=w_out[:o1].astype(BF16), w_out_o=w_out[o1:].astype(BF16))


def _mixer_ab(h, w, g_q, g_kv, batch, tables, ckv_ctx=None, kr_ctx=None):
    t = h.shape[0]
    n = t // batch
    uv = _matmul([h], [w["w_uv"]], F32, epilogue="gelu", name="ab_uv_proj")
    a = _spatial_gate(uv, w["w_s"], w["b_s_t"])
    qn, ckv, kr = _ab_small(h, w["w_small"], g_q, g_kv, tables)
    q_scale = MLA_SCALE * LOG2E
    if tables is None:
        q_cat = _matmul([qn], [w["w_q"]], BF16, scale=q_scale, name="mla_q_proj")
    else:
        q_cat = _matmul([qn], [w["w_q"]], BF16, epilogue="rope", tables=tables, rope_shift=MLA_ROPE // 4,
                        rope_groups="odd", scale=q_scale, name="mla_q_proj")
    ckv_all = ckv.astype(BF16).reshape(batch, n, KV_LORA)
    kr_all = kr.astype(BF16).reshape(batch, n, LANES)
    if ckv_ctx is not None:
        kr_pad = jnp.pad(kr_ctx.astype(BF16), ((0, 0), (0, 0), (0, LANES - MLA_ROPE)))
        ckv_all = jnp.concatenate([ckv_all, ckv_ctx.astype(BF16)], axis=1)
        kr_all = jnp.concatenate([kr_all, kr_pad], axis=1)
    nk = ckv_all.shape[1]
    k_cat, v = _mla_kv(ckv_all.reshape(batch * nk, KV_LORA), kr_all.reshape(batch * nk, LANES), w["w_kv"])
    o = _mla_attention(q_cat, k_cat, v, batch)
    y = _matmul([a, o], [w["w_out_a"], w["w_out_o"]], F32, name="ab_out_proj")
    return y, ckv, kr


def _mixer_c(h, w_in, sink, w_out, batch, tables, k_ctx=None, v_ctx=None):
    if tables is None:
        z = _matmul([h], [w_in], F32, name="swa_in_proj")
        o = _swa_context(z, sink, batch)
    else:
        z = _matmul([h], [w_in], BF16, epilogue="rope", tables=tables, rope_shift=HEAD_DIM // 4,
                    rope_cols=C_OUT + C_KV, scale=SWA_SCALE * LOG2E, scale_cols=C_OUT, name="swa_in_proj")
        o = _swa_latent(z, k_ctx, v_ctx, sink, batch)
    return _matmul([o], [w_out], F32, name="swa_out_proj"), z


def _trunk(x, batch, mods, row0, rows_per_mod, prm, caches):
    depth = mods.shape[0]
    n = x.shape[0] // batch
    latent = caches is not None
    tab_mla = _rope_tables(n, MLA_ROPE) if latent else None
    tab_swa = _rope_tables(n, HEAD_DIM) if latent else None
    new = dict(ckv=[], kr=[], k=[], v=[])
    post = functools.partial(_post, rows_per_mod=rows_per_mod, row0=row0)
    h = _pre(x, mods[0], prm["norm_pre"][0, 0], 0, rows_per_mod, row0)
    for l in range(depth):
        j = l // 2
        m = mods[l]
        y = _ffn(h, prm, 2 * l)
        x, h = post(x, y, m, prm["norm_post"][l, 0], 2, MACARON, mods_next=m, g_next=prm["norm_pre"][l, 1],
                    base_next=3)
        if l % 2 == 0:
            ckv_ctx = caches["mla_ckv"][:, j] if latent else None
            kr_ctx = caches["mla_krope"][:, j] if latent else None
            y, ckv, kr = _mixer_ab(h, prm["ab"][j], prm["g_q"][j], prm["g_kv"][j], batch, tab_mla, ckv_ctx, kr_ctx)
            new["ckv"].append(ckv.reshape(batch, n, KV_LORA))
            new["kr"].append(kr[:, :MLA_ROPE].reshape(batch, n, MLA_ROPE))
        else:
            past_k = caches["swa_k"][:, j].reshape(batch, -1, C_KV) if latent else None
            past_v = caches["swa_v"][:, j].reshape(batch, -1, C_KV) if latent else None
            y, z = _mixer_c(h, prm["w_in_c"][j], prm["sink_c"][j], prm["w_out_c"][j], batch, tab_swa, past_k, past_v)
            new["k"].append(z[:, C_OUT:C_OUT + C_KV].reshape(batch, n, SWA_KV_HEADS, HEAD_DIM))
            new["v"].append(z[:, C_OUT + C_KV:].reshape(batch, n, SWA_KV_HEADS, HEAD_DIM))
        x, h = post(x, y, m, prm["norm_post"][l, 1], 5, 1.0, mods_next=m, g_next=prm["norm_pre"][l, 2], base_next=6)
        y = _ffn(h, prm, 2 * l + 1)
        if l + 1 < depth:
            x, h = post(x, y, m, prm["norm_post"][l, 2], 8, MACARON, mods_next=mods[l + 1],
                        g_next=prm["norm_pre"][l + 1, 0], base_next=0)
        else:
            x, _ = post(x, y, m, prm["norm_post"][l, 2], 8, MACARON)
    return x, new


def kernel(x_prompt, x_sample, cache_mla_ckv, cache_mla_krope, cache_swa_k, cache_swa_v, c, c_ctx, w_mod, b_mod,
           norm_pre, norm_post, w_ffn_gate, w_ffn_up, w_ffn_down, w_in_ab, g_q, g_kv, w_uq, w_ukv, w_s, b_s,
           w_out_ab, w_in_c, sink_c, w_out_c):
    depth = w_mod.shape[0]
    batch, seq, d = x_prompt.shape
    dec_batch, dec_seq, _ = x_sample.shape
    assert 1 + dec_batch <= MOD_ROWS

    cond = jnp.concatenate([c_ctx[None, :], c], axis=0)
    cond = jnp.pad(cond, ((0, MOD_ROWS - cond.shape[0]), (0, 0)))
    mods = _ada_mods(cond, w_mod, b_mod)

    ff = w_ffn_gate.shape[-1]
    ffp = -(-ff // FF_PAD) * FF_PAD
    prm = dict(
        norm_pre=norm_pre, norm_post=norm_post, g_q=g_q, g_kv=g_kv, sink_c=sink_c,
        ffn=(_cast_pad_cols(w_ffn_gate.reshape(2 * depth, d, ff), ffp),
             _cast_pad_cols(w_ffn_up.reshape(2 * depth, d, ff), ffp),
             _cast_pad_rows(w_ffn_down.reshape(2 * depth, ff, d), ffp)),
        ab=[_ab_weights(w_in_ab[j], w_uq[j], w_ukv[j], w_s[j], b_s[j], w_out_ab[j]) for j in range(w_in_ab.shape[0])],
        w_in_c=[w_in_c[j].astype(BF16) for j in range(w_in_c.shape[0])],
        w_out_c=[w_out_c[j].astype(BF16) for j in range(w_out_c.shape[0])],
    )

    y_prompt, new = _trunk(x_prompt.reshape(batch * seq, d), batch, mods, 0, batch * seq, prm, None)
    caches = dict(mla_ckv=cache_mla_ckv, mla_krope=cache_mla_krope, swa_k=cache_swa_k, swa_v=cache_swa_v)
    y_sample, _ = _trunk(x_sample.reshape(dec_batch * dec_seq, d), dec_batch, mods, 1, dec_seq, prm, caches)

    return (y_prompt.reshape(batch, seq, d), y_sample.reshape(dec_batch, dec_seq, d),
            jnp.stack(new["ckv"], axis=1), jnp.stack(new["kr"], axis=1),
            jnp.stack(new["k"], axis=1), jnp.stack(new["v"], axis=1))
```

```python
import functools

import jax
import jax.numpy as jnp
from jax import lax
from jax.experimental import pallas as pl
from jax.experimental.pallas import tpu as pltpu

F32 = jnp.float32
BF16 = jnp.bfloat16

GRID_W = 64
CHUNK = 128
A_GROUPS = 16
A_DIM = 128
A_WIDTH = A_GROUPS * A_DIM
MLA_HEADS = 16
MLA_NOPE = 128
MLA_ROPE = 64
MLA_V = 128
Q_LORA = 1024
KV_LORA = 512
MLA_SCALE = (MLA_NOPE + MLA_ROPE) ** -0.5
SWA_HEADS = 32
SWA_KV_HEADS = 8
SWA_GROUP = SWA_HEADS // SWA_KV_HEADS
HEAD_DIM = 128
WINDOW = 128
SWA_SCALE = HEAD_DIM ** -0.5
MACARON = 0.5
ROPE_BASE = 10000.0
LOG2E = 1.4426950408889634
EPS = 1e-6
N_MOD = 9
C_KV = SWA_KV_HEADS * HEAD_DIM
C_OUT = SWA_HEADS * HEAD_DIM

LANES = 128
MXU_DEPTH = 256
V7X_VMEM_BYTES = 64 * 1024 * 1024
VMEM_REQUEST_CAP = V7X_VMEM_BYTES - 6 * 1024 * 1024
MOD_ROWS = 16
FFN_TN = 512
FFN_DOWN_TM = 512
SWA_TQ = 1024
SWA_SUB = 128
MLA_TQ = 1024
MLA_KEY_CHUNK = 1088


def _pick(n, prefs):
    for p in prefs:
        if p <= n and n % p == 0:
            return p
    return n


def _params(semantics, block_bytes, temp_bytes=0):
    need = 2 * block_bytes + temp_bytes + (2 << 20)
    return pltpu.CompilerParams(dimension_semantics=semantics,
                                vmem_limit_bytes=int(min(max(need, 16 << 20), VMEM_REQUEST_CAP)))


def _rms(x, g):
    return x * lax.rsqrt(jnp.mean(x * x, axis=-1, keepdims=True) + EPS) * g


def _mod_kernel(c_ref, w_ref, b_ref, o_ref):
    c = c_ref[...]
    s = (c * jax.nn.sigmoid(c)).astype(BF16)
    o_ref[...] = jnp.dot(s, w_ref[...].astype(BF16), preferred_element_type=F32) + b_ref[...]


def _ada_mods(cond, w_mod, b_mod):
    depth, d, n = w_mod.shape
    tn = _pick(n, (1024, 512, 256, 128))
    out = pl.pallas_call(
        _mod_kernel,
        grid=(depth, n // tn),
        in_specs=[pl.BlockSpec((MOD_ROWS, d), lambda l, j: (0, 0)),
                  pl.BlockSpec((None, d, tn), lambda l, j: (l, 0, j)),
                  pl.BlockSpec((None, 1, tn), lambda l, j: (l, 0, j))],
        out_specs=pl.BlockSpec((None, MOD_ROWS, tn), lambda l, j: (l, 0, j)),
        out_shape=jax.ShapeDtypeStruct((depth, MOD_ROWS, n), F32),
        compiler_params=_params(("arbitrary", "arbitrary"), MOD_ROWS * d * 4 + d * tn * 4 + MOD_ROWS * tn * 4,
                                d * tn * 2),
        name="ada_mods",
    )(cond, w_mod, b_mod.reshape(depth, 1, n))
    return out.reshape(depth, MOD_ROWS, N_MOD, d)


def _pre_kernel(x_ref, m_ref, g_ref, h_ref, *, base):
    n = _rms(x_ref[...], g_ref[...])
    h_ref[...] = (n * (1.0 + m_ref[base + 1:base + 2, :]) + m_ref[base:base + 1, :]).astype(BF16)


def _post_kernel(x_ref, y_ref, mc_ref, gp_ref, *rest, gate, weight, base_next):
    x = x_ref[...] + weight * mc_ref[gate:gate + 1, :] * _rms(y_ref[...], gp_ref[...])
    if base_next is None:
        (xo_ref,) = rest
    else:
        mn_ref, gn_ref, xo_ref, h_ref = rest
        n = _rms(x, gn_ref[...])
        h_ref[...] = (n * (1.0 + mn_ref[base_next + 1:base_next + 2, :])
                      + mn_ref[base_next:base_next + 1, :]).astype(BF16)
    xo_ref[...] = x


def _row_specs(t, d, tm, rows_per_mod, row0):
    row = pl.BlockSpec((tm, d), lambda i: (i, 0))
    mod = pl.BlockSpec((None, N_MOD, d), lambda i: (row0 + (i * tm) // rows_per_mod, 0, 0))
    gain = pl.BlockSpec((1, d), lambda i: (0, 0))
    return row, mod, gain


def _pre(x, mods, g_pre, base, rows_per_mod, row0):
    t, d = x.shape
    tm = _pick(min(t, rows_per_mod), (256, 128))
    row, mod, gain = _row_specs(t, d, tm, rows_per_mod, row0)
    return pl.pallas_call(
        functools.partial(_pre_kernel, base=base),
        grid=(t // tm,),
        in_specs=[row, mod, gain],
        out_specs=row,
        out_shape=jax.ShapeDtypeStruct((t, d), BF16),
        compiler_params=_params(("arbitrary",), tm * d * 6 + N_MOD * d * 8, 3 * tm * d * 4),
        name="pre_norm",
    )(x, mods, g_pre.reshape(1, d))


def _post(x, y, mods_cur, g_post, gate, weight, rows_per_mod, row0, mods_next=None, g_next=None, base_next=None):
    t, d = x.shape
    tm = _pick(min(t, rows_per_mod), (256, 128))
    row, mod, gain = _row_specs(t, d, tm, rows_per_mod, row0)
    kern = functools.partial(_post_kernel, gate=gate, weight=weight, base_next=base_next)
    if base_next is None:
        return pl.pallas_call(
            kern, grid=(t // tm,),
            in_specs=[row, row, mod, gain],
            out_specs=row,
            out_shape=jax.ShapeDtypeStruct((t, d), F32),
            compiler_params=_params(("arbitrary",), 3 * tm * d * 4 + N_MOD * d * 4, 3 * tm * d * 4),
            name="post_norm",
        )(x, y, mods_cur, g_post.reshape(1, d)), None
    return pl.pallas_call(
        kern, grid=(t // tm,),
        in_specs=[row, row, mod, gain, mod, gain],
        out_specs=[row, row],
        out_shape=[jax.ShapeDtypeStruct((t, d), F32), jax.ShapeDtypeStruct((t, d), BF16)],
        compiler_params=_params(("arbitrary",), 3 * tm * d * 4 + tm * d * 2 + 2 * N_MOD * d * 4, 4 * tm * d * 4),
        name="post_pre_norm",
    )(x, y, mods_cur, g_post.reshape(1, d), mods_next, g_next.reshape(1, d))


def _rope(x, cos_ref, s_up_ref, s_dn_ref, shift):
    return (x * cos_ref[...] + pltpu.roll(x, LANES - shift, 1) * s_up_ref[...]
            + pltpu.roll(x, shift, 1) * s_dn_ref[...])


def _mm_kernel(*refs, n_a, n_b, epilogue, rope_shift, rope_groups, scale, scale_blocks):
    a_refs, b_refs = refs[:n_a], refs[n_a:n_a + n_b]
    o_ref = refs[-1]
    if epilogue == "swiglu":
        g = jnp.dot(a_refs[0][...], b_refs[0][...], preferred_element_type=F32)
        u = jnp.dot(a_refs[0][...], b_refs[1][...], preferred_element_type=F32)
        o_ref[...] = (g * jax.nn.sigmoid(g) * u).astype(o_ref.dtype)
        return
    acc = jnp.dot(a_refs[0][...], b_refs[0][...], preferred_element_type=F32)
    for a_ref, b_ref in zip(a_refs[1:], b_refs[1:]):
        acc = acc + jnp.dot(a_ref[...], b_ref[...], preferred_element_type=F32)
    if scale is not None:
        if scale_blocks is None:
            acc = acc * scale
        else:
            acc = acc * jnp.where(pl.program_id(1) < scale_blocks, scale, 1.0).astype(F32)
    if epilogue == "gelu":
        o_ref[...] = jax.nn.gelu(acc, approximate=True).astype(o_ref.dtype)
    elif epilogue == "rope":
        cos_ref, s_up_ref, s_dn_ref = refs[n_a + n_b:n_a + n_b + 3]
        for c in range(acc.shape[1] // LANES):
            x = acc[:, c * LANES:(c + 1) * LANES]
            if rope_groups == "all" or c % 2 == 1:
                x = _rope(x, cos_ref, s_up_ref, s_dn_ref, rope_shift)
            o_ref[:, c * LANES:(c + 1) * LANES] = x.astype(o_ref.dtype)
    else:
        o_ref[...] = acc.astype(o_ref.dtype)


def _b_spec(b, tn, col0):
    if not isinstance(b, tuple):
        return b, b.shape[0], pl.BlockSpec((b.shape[0], tn), lambda i, j: (0, col0 + j))
    if len(b) == 2:
        arr, layer = b
        return arr, arr.shape[1], pl.BlockSpec((None, arr.shape[1], tn), lambda i, j: (layer, 0, col0 + j))
    arr, layer, rows, row_block = b
    return arr, rows, pl.BlockSpec((None, rows, tn), lambda i, j: (layer, row_block, col0 + j))


def _matmul(a_list, b_list, out_dtype, epilogue=None, tables=None, rope_shift=None, rope_groups="all",
            rope_cols=None, scale=None, scale_cols=None, tm=None, tn=None, n_cols=None, col0=0, name="matmul"):
    m = a_list[0].shape[0]
    n = n_cols or (b_list[0][0] if isinstance(b_list[0], tuple) else b_list[0]).shape[-1]
    tm = _pick(m, (tm or 1024, 512, 256, 128))
    tn = _pick(n, (tn or 1024, 512, 256, 128))
    in_specs = [pl.BlockSpec((tm, a.shape[1]), lambda i, j: (i, 0)) for a in a_list]
    args = list(a_list)
    k_total = 0
    for b in b_list:
        arr, k, spec = _b_spec(b, tn, col0)
        in_specs.append(spec)
        args.append(arr)
        k_total += k
    block_bytes = (tm * sum(a.shape[1] for a in a_list) + k_total * tn) * 2 + tm * tn * jnp.dtype(out_dtype).itemsize
    if epilogue == "rope":
        n_pos = tables[0].shape[0] // 2
        tm = min(tm, n_pos)
        in_specs[:len(a_list)] = [pl.BlockSpec((tm, a.shape[1]), lambda i, j: (i, 0)) for a in a_list]
        pos_blocks = n_pos // tm
        rope_blocks = (n if rope_cols is None else rope_cols) // tn
        assert rope_blocks * tn == (n if rope_cols is None else rope_cols)
        in_specs += [pl.BlockSpec(
            (tm, LANES), lambda i, j: (i % pos_blocks + jnp.where(j < rope_blocks, 0, pos_blocks), 0))] * 3
        args += list(tables)
        block_bytes += 3 * tm * LANES * 4
    scale_blocks = None
    if scale_cols is not None and scale_cols < n:
        assert scale_cols % tn == 0
        scale_blocks = scale_cols // tn
    kern = functools.partial(_mm_kernel, n_a=len(a_list), n_b=len(b_list), epilogue=epilogue, rope_shift=rope_shift,
                             rope_groups=rope_groups, scale=scale, scale_blocks=scale_blocks)
    return pl.pallas_call(
        kern, grid=(m // tm, n // tn),
        in_specs=in_specs,
        out_specs=pl.BlockSpec((tm, tn), lambda i, j: (i, j)),
        out_shape=jax.ShapeDtypeStruct((m, n), out_dtype),
        compiler_params=_params(("arbitrary", "arbitrary"), block_bytes, 4 * len(b_list) * tm * tn * 4),
        name=name,
    )(*args)


def _cast_kernel(x_ref, o_ref):
    o_ref[...] = x_ref[...].astype(BF16)


def _cast_bf16(w):
    layers, r, n = w.shape
    tr = _pick(r, (128,))
    spec = pl.BlockSpec((None, tr, n), lambda l, i: (l, i, 0))
    return pl.pallas_call(
        _cast_kernel, grid=(layers, r // tr),
        in_specs=[spec], out_specs=spec,
        out_shape=jax.ShapeDtypeStruct((layers, r, n), BF16),
        compiler_params=_params(("arbitrary", "arbitrary"), tr * n * 6, tr * n * 4),
        name="cast_bf16",
    )(w)


def _ab_small_kernel(h_ref, w_ref, gq_ref, gkv_ref, *rest, rope):
    z = jnp.dot(h_ref[...], w_ref[...], preferred_element_type=F32)
    if rope:
        cos_ref, s_up_ref, s_dn_ref, qn_ref, ckv_ref, kr_ref = rest
    else:
        qn_ref, ckv_ref, kr_ref = rest
    qn_ref[...] = _rms(z[:, :Q_LORA], gq_ref[...]).astype(BF16)
    ckv_ref[...] = _rms(z[:, Q_LORA:Q_LORA + KV_LORA], gkv_ref[...])
    kr = z[:, Q_LORA + KV_LORA:]
    if rope:
        kr = _rope(kr, cos_ref, s_up_ref, s_dn_ref, MLA_ROPE // 4)
    kr_ref[...] = kr


def _ab_small(h, w_small, g_q, g_kv, tables):
    t, d = h.shape
    n = w_small.shape[1]
    tm = _pick(t, (512, 256, 128))
    rope = tables is not None
    in_specs = [pl.BlockSpec((tm, d), lambda i: (i, 0)),
                pl.BlockSpec((d, n), lambda i: (0, 0)),
                pl.BlockSpec((1, Q_LORA), lambda i: (0, 0)),
                pl.BlockSpec((1, KV_LORA), lambda i: (0, 0))]
    args = [h, w_small, g_q.reshape(1, Q_LORA), g_kv.reshape(1, KV_LORA)]
    if rope:
        n_pos = tables[0].shape[0] // 2
        tm = min(tm, n_pos)
        in_specs[0] = pl.BlockSpec((tm, d), lambda i: (i, 0))
        pos_blocks = n_pos // tm
        in_specs += [pl.BlockSpec((tm, LANES), lambda i: (i % pos_blocks, 0))] * 3
        args += list(tables)
    return pl.pallas_call(
        functools.partial(_ab_small_kernel, rope=rope), grid=(t // tm,),
        in_specs=in_specs,
        out_specs=[pl.BlockSpec((tm, Q_LORA), lambda i: (i, 0)),
                   pl.BlockSpec((tm, KV_LORA), lambda i: (i, 0)),
                   pl.BlockSpec((tm, LANES), lambda i: (i, 0))],
        out_shape=[jax.ShapeDtypeStruct((t, Q_LORA), BF16),
                   jax.ShapeDtypeStruct((t, KV_LORA), F32),
                   jax.ShapeDtypeStruct((t, LANES), F32)],
        compiler_params=_params(("arbitrary",), (tm * d + d * n) * 2 + tm * n * 4, 3 * tm * n * 4),
        name="ab_small_proj",
    )(*args)


def _mla_kv_kernel(ckv_ref, kr_ref, w_ref, k_ref, v_ref):
    kv = jnp.dot(ckv_ref[...], w_ref[...], preferred_element_type=F32)
    kr = kr_ref[...]
    width = MLA_HEADS * MLA_NOPE
    for h in range(MLA_HEADS):
        k_ref[:, h * MXU_DEPTH:h * MXU_DEPTH + MLA_NOPE] = kv[:, h * MLA_NOPE:(h + 1) * MLA_NOPE].astype(BF16)
        k_ref[:, h * MXU_DEPTH + MLA_NOPE:(h + 1) * MXU_DEPTH] = kr
    v_ref[...] = kv[:, width:].astype(BF16)


def _mla_kv(ckv, kr, w_kv):
    t = ckv.shape[0]
    tm = _pick(t, (512, 256, 128))
    kw = MLA_HEADS * MXU_DEPTH
    vw = MLA_HEADS * MLA_V
    return pl.pallas_call(
        _mla_kv_kernel, grid=(t // tm,),
        in_specs=[pl.BlockSpec((tm, KV_LORA), lambda i: (i, 0)),
                  pl.BlockSpec((tm, LANES), lambda i: (i, 0)),
                  pl.BlockSpec(w_kv.shape, lambda i: (0, 0))],
        out_specs=[pl.BlockSpec((tm, kw), lambda i: (i, 0)),
                   pl.BlockSpec((tm, vw), lambda i: (i, 0))],
        out_shape=[jax.ShapeDtypeStruct((t, kw), BF16), jax.ShapeDtypeStruct((t, vw), BF16)],
        compiler_params=_params(("arbitrary",), (tm * (KV_LORA + LANES + kw + vw) + w_kv.size) * 2,
                                2 * tm * w_kv.shape[1] * 4),
        name="mla_kv_up",
    )(ckv, kr, w_kv)


def _dot_nt(a, b):
    return lax.dot_general(a, b, (((1,), (1,)), ((), ())), preferred_element_type=F32)


def _mla_attn_kernel(q_ref, k_ref, v_ref, o_ref, *, bounds):
    q = q_ref[...]
    m = l = o = None
    for lo, hi in bounds:
        s = _dot_nt(q, k_ref[lo:hi, :])
        m_c = jnp.max(s, axis=-1, keepdims=True)
        if m is None:
            m = m_c
            p = jnp.exp2(s - m)
            l = jnp.sum(p, axis=-1, keepdims=True)
            o = jnp.dot(p.astype(BF16), v_ref[lo:hi, :], preferred_element_type=F32)
        else:
            m_new = jnp.maximum(m, m_c)
            alpha = jnp.exp2(m - m_new)
            p = jnp.exp2(s - m_new)
            l = l * alpha + jnp.sum(p, axis=-1, keepdims=True)
            o = o * alpha + jnp.dot(p.astype(BF16), v_ref[lo:hi, :], preferred_element_type=F32)
            m = m_new
    o_ref[...] = (o / l).astype(BF16)


def _key_chunks(nk, target):
    groups = nk // LANES
    parts = max(1, round(nk / target))
    sizes = [groups // parts + (1 if c < groups % parts else 0) for c in range(parts)]
    edges = [0]
    for sz in sizes:
        edges.append(edges[-1] + sz * LANES)
    return tuple((edges[c], edges[c + 1]) for c in range(parts))


def _mla_attention(q_cat, k_cat, v, batch):
    nq = q_cat.shape[0] // batch
    nk = k_cat.shape[0] // batch
    tq = _pick(nq, (MLA_TQ, 512, 256, 128))
    qb = nq // tq
    return pl.pallas_call(
        functools.partial(_mla_attn_kernel, bounds=_key_chunks(nk, MLA_KEY_CHUNK)), grid=(batch, MLA_HEADS, qb),
        in_specs=[pl.BlockSpec((tq, MXU_DEPTH), lambda b, h, i: (b * qb + i, h)),
                  pl.BlockSpec((nk, MXU_DEPTH), lambda b, h, i: (b, h)),
                  pl.BlockSpec((nk, MLA_V), lambda b, h, i: (b, h))],
        out_specs=pl.BlockSpec((tq, MLA_V), lambda b, h, i: (b * qb + i, h)),
        out_shape=jax.ShapeDtypeStruct((batch * nq, MLA_HEADS * MLA_V), BF16),
        compiler_params=_params(("arbitrary", "arbitrary", "arbitrary"),
                                (tq * MXU_DEPTH + nk * MXU_DEPTH + nk * MLA_V + tq * MLA_V) * 2, 4 * tq * nk * 4),
        name="mla_attention",
    )(q_cat, k_cat, v)


def _sink_softmax_pv(scores, values, sink):
    m = sink
    for s in scores:
        m = jnp.maximum(m, jnp.max(s, axis=-1, keepdims=True))
    o = None
    for s, v in zip(scores, values):
        pv = jnp.dot(jnp.exp2(s - m).astype(BF16), v, preferred_element_type=F32)
        o = pv if o is None else o + pv
    return o[:, :HEAD_DIM] / (o[:, HEAD_DIM:] + jnp.exp2(sink - m))


def _with_ones(v):
    return jnp.concatenate([v, jnp.ones(v.shape, v.dtype)], axis=1)


def _swa_lat_kernel(sink_ref, q_ref, k_ref, v_ref, kc_ref, vc_ref, o_ref, *, sub, win, n):
    kvh = pl.program_id(1)
    tq = q_ref.shape[0]
    kc = kc_ref[...].astype(BF16)
    vc = _with_ones(vc_ref[...].astype(BF16))
    neg = jnp.finfo(F32).min
    for r in range(tq // sub):
        st = pl.program_id(2) * tq + r * sub
        ws = pl.multiple_of(jnp.clip(st - WINDOW, 0, n - win), LANES)
        kw = k_ref[pl.ds(ws, win), :]
        vw = _with_ones(v_ref[pl.ds(ws, win), :])
        qpos = st + lax.broadcasted_iota(jnp.int32, (sub, win), 0)
        kpos = ws + lax.broadcasted_iota(jnp.int32, (sub, win), 1)
        band = jnp.abs(qpos - kpos) <= WINDOW
        for g in range(SWA_GROUP):
            q = q_ref[r * sub:(r + 1) * sub, g * HEAD_DIM:(g + 1) * HEAD_DIM]
            s_loc = jnp.where(band, _dot_nt(q, kw), neg)
            s_ctx = _dot_nt(q, kc)
            sink = jnp.full((sub, 1), sink_ref[kvh * SWA_GROUP + g] * LOG2E, F32)
            o = _sink_softmax_pv([s_loc, s_ctx], [vw, vc], sink)
            o_ref[r * sub:(r + 1) * sub, g * HEAD_DIM:(g + 1) * HEAD_DIM] = o.astype(BF16)


def _swa_latent(z, k_ctx, v_ctx, sink, batch):
    n = z.shape[0] // batch
    past = k_ctx.shape[1]
    tq = _pick(n, (SWA_TQ, 128))
    sub = min(tq, SWA_SUB)
    win = min(n, sub + 2 * WINDOW)
    qb = n // tq
    qw = SWA_GROUP * HEAD_DIM
    k_col = C_OUT // HEAD_DIM
    v_col = (C_OUT + C_KV) // HEAD_DIM
    kern = functools.partial(_swa_lat_kernel, sub=sub, win=win, n=n)
    return pl.pallas_call(
        kern, grid=(batch, SWA_KV_HEADS, qb),
        in_specs=[pl.BlockSpec(memory_space=pltpu.SMEM),
                  pl.BlockSpec((tq, qw), lambda b, h, i: (b * qb + i, h)),
                  pl.BlockSpec((n, HEAD_DIM), lambda b, h, i: (b, k_col + h)),
                  pl.BlockSpec((n, HEAD_DIM), lambda b, h, i: (b, v_col + h)),
                  pl.BlockSpec((None, past, HEAD_DIM), lambda b, h, i: (b, 0, h)),
                  pl.BlockSpec((None, past, HEAD_DIM), lambda b, h, i: (b, 0, h))],
        out_specs=pl.BlockSpec((tq, qw), lambda b, h, i: (b * qb + i, h)),
        out_shape=jax.ShapeDtypeStruct((batch * n, C_OUT), BF16),
        compiler_params=_params(("arbitrary", "arbitrary", "arbitrary"),
                                (2 * tq * qw + 2 * n * HEAD_DIM) * 2 + 2 * past * HEAD_DIM * 4,
                                6 * tq * (win + past) * 4),
        name="swa_latent_attention",
    )(sink, z, z, z, k_ctx, v_ctx)


def _swa_ctx_kernel(sink_ref, q_ref, k_ref, v_ref, o_ref):
    kvh = pl.program_id(1)
    k = k_ref[...].astype(BF16)
    v = _with_ones(v_ref[...].astype(BF16))
    tq = q_ref.shape[0]
    for g in range(SWA_GROUP):
        q = (q_ref[:, g * HEAD_DIM:(g + 1) * HEAD_DIM] * (SWA_SCALE * LOG2E)).astype(BF16)
        s = _dot_nt(q, k)
        sink = jnp.full((tq, 1), sink_ref[kvh * SWA_GROUP + g] * LOG2E, F32)
        o_ref[:, g * HEAD_DIM:(g + 1) * HEAD_DIM] = _sink_softmax_pv([s], [v], sink).astype(BF16)


def _swa_context(z, sink, batch):
    n = z.shape[0] // batch
    qw = SWA_GROUP * HEAD_DIM
    k_col = C_OUT // HEAD_DIM
    v_col = (C_OUT + C_KV) // HEAD_DIM
    return pl.pallas_call(
        _swa_ctx_kernel, grid=(batch, SWA_KV_HEADS),
        in_specs=[pl.BlockSpec(memory_space=pltpu.SMEM),
                  pl.BlockSpec((n, qw), lambda b, h: (b, h)),
                  pl.BlockSpec((n, HEAD_DIM), lambda b, h: (b, k_col + h)),
                  pl.BlockSpec((n, HEAD_DIM), lambda b, h: (b, v_col + h))],
        out_specs=pl.BlockSpec((n, qw), lambda b, h: (b, h)),
        out_shape=jax.ShapeDtypeStruct((batch * n, C_OUT), BF16),
        compiler_params=_params(("arbitrary", "arbitrary"), n * qw * 6 + 2 * n * HEAD_DIM * 4, 6 * n * n * 4),
        name="swa_context_attention",
    )(sink, z, z, z)


def _gate_kernel(uv_ref, ws_ref, bs_ref, a_ref, *, chunks):
    for c in range(chunks):
        rows = slice(c * CHUNK, (c + 1) * CHUNK)
        v = uv_ref[rows, A_WIDTH:]
        mu = jnp.mean(v, axis=-1, keepdims=True)
        vc = v - mu
        vn = (vc * lax.rsqrt(jnp.mean(vc * vc, axis=-1, keepdims=True) + EPS)).astype(BF16)
        for g in range(A_GROUPS):
            cols = slice(g * A_DIM, (g + 1) * A_DIM)
            mixed = jnp.dot(ws_ref[g], vn[:, cols], preferred_element_type=F32) + bs_ref[:, g:g + 1]
            a_ref[rows, cols] = (uv_ref[rows, cols] * mixed).astype(BF16)


def _spatial_gate(uv, w_s, b_s_t):
    t = uv.shape[0]
    tm = _pick(t, (256, 128))
    return pl.pallas_call(
        functools.partial(_gate_kernel, chunks=tm // CHUNK), grid=(t // tm,),
        in_specs=[pl.BlockSpec((tm, 2 * A_WIDTH), lambda i: (i, 0)),
                  pl.BlockSpec(w_s.shape, lambda i: (0, 0, 0)),
                  pl.BlockSpec(b_s_t.shape, lambda i: (0, 0))],
        out_specs=pl.BlockSpec((tm, A_WIDTH), lambda i: (i, 0)),
        out_shape=jax.ShapeDtypeStruct((t, A_WIDTH), BF16),
        compiler_params=_params(("arbitrary",), tm * 2 * A_WIDTH * 4 + tm * A_WIDTH * 2 + w_s.size * 2,
                                4 * CHUNK * A_WIDTH * 4),
        name="spatial_gate",
    )(uv, w_s, b_s_t)


def _rope_tables(n, rot_dim):
    rows = n // GRID_W
    pos_r = jnp.repeat(jnp.arange(rows, dtype=F32), GRID_W)
    pos_c = jnp.tile(jnp.arange(GRID_W, dtype=F32), rows)
    half = rot_dim // 2
    quarter = half // 2
    inv_freq = 1.0 / (ROPE_BASE ** (jnp.arange(0, half, 2, dtype=F32) / half))
    ang_r = pos_r[:, None] * inv_freq[None, :]
    ang_c = pos_c[:, None] * inv_freq[None, :]
    cos = jnp.concatenate([jnp.cos(ang_r), jnp.cos(ang_r), jnp.cos(ang_c), jnp.cos(ang_c)], axis=-1)
    sin = jnp.concatenate([jnp.sin(ang_r), jnp.sin(ang_r), jnp.sin(ang_c), jnp.sin(ang_c)], axis=-1)
    first = (jnp.arange(rot_dim) % half) < quarter
    s_up = jnp.where(first[None, :], -sin, 0.0)
    s_dn = jnp.where(first[None, :], 0.0, sin)
    pad = LANES - rot_dim
    if pad:
        cos = jnp.pad(cos, ((0, 0), (0, pad)), constant_values=1.0)
        s_up = jnp.pad(s_up, ((0, 0), (0, pad)))
        s_dn = jnp.pad(s_dn, ((0, 0), (0, pad)))
    zeros = jnp.zeros_like(cos)
    return (jnp.concatenate([cos, zeros + 1.0], axis=0), jnp.concatenate([s_up, zeros], axis=0),
            jnp.concatenate([s_dn, zeros], axis=0))


def _ffn(h, prm, idx):
    w_gate, w_up, w_down = prm["ffn"]
    ff = w_gate.shape[-1]
    main = ff // FFN_TN * FFN_TN
    tail = ff - main
    gate_up = [(w_gate, idx), (w_up, idx)]
    parts = [_matmul([h], gate_up, BF16, epilogue="swiglu", tn=FFN_TN, n_cols=main, name="ffn_gate_up")]
    slabs = [(w_down, idx, main, 0)]
    if tail:
        assert tail % LANES == 0 and main % tail == 0
        parts.append(_matmul([h], gate_up, BF16, epilogue="swiglu", tn=tail, n_cols=tail, col0=main // tail,
                             name="ffn_gate_up_tail"))
        slabs.append((w_down, idx, tail, main // tail))
    return _matmul(parts, slabs, F32, tm=FFN_DOWN_TM, tn=FFN_TN, name="ffn_down")


def _ab_weights(w_in, w_uq, w_ukv, w_s, b_s, w_out):
    o1, o2 = A_WIDTH, 2 * A_WIDTH
    w_uv = w_in[:, :o2].astype(BF16)
    w_small = jnp.pad(w_in[:, o2:], ((0, 0), (0, LANES - MLA_ROPE))).astype(BF16)
    uq = w_uq.reshape(Q_LORA, MLA_HEADS, MLA_NOPE + MLA_ROPE)
    uq = jnp.pad(uq, ((0, 0), (0, 0), (0, MXU_DEPTH - MLA_NOPE - MLA_ROPE)))
    w_q = uq.reshape(Q_LORA, MLA_HEADS * MXU_DEPTH).astype(BF16)
    ukv = w_ukv.reshape(KV_LORA, MLA_HEADS, MLA_NOPE + MLA_V)
    w_kv = jnp.concatenate([ukv[:, :, :MLA_NOPE].reshape(KV_LORA, -1),
                            ukv[:, :, MLA_NOPE:].reshape(KV_LORA, -1)], axis=1).astype(BF16)
    return dict(w_uv=w_uv, w_small=w_small, w_q=w_q, w_kv=w_kv, w_s=w_s.astype(BF16),
                b_s_t=jnp.transpose(b_s), w_out_a=w_out[:o1].astype(BF16), w_out_o=w_out[o1:].astype(BF16))


def _mixer_ab(h, w, g_q, g_kv, batch, tables, ckv_ctx=None, kr_ctx=None):
    t = h.shape[0]
    n = t // batch
    uv = _matmul([h], [w["w_uv"]], F32, epilogue="gelu", name="ab_uv_proj")
    a = _spatial_gate(uv, w["w_s"], w["b_s_t"])
    qn, ckv, kr = _ab_small(h, w["w_small"], g_q, g_kv, tables)
    q_scale = MLA_SCALE * LOG2E
    if tables is None:
        q_cat = _matmul([qn], [w["w_q"]], BF16, scale=q_scale, name="mla_q_proj")
    else:
        q_cat = _matmul([qn], [w["w_q"]], BF16, epilogue="rope", tables=tables, rope_shift=MLA_ROPE // 4,
                        rope_groups="odd", scale=q_scale, name="mla_q_proj")
    ckv_all = ckv.astype(BF16).reshape(batch, n, KV_LORA)
    kr_all = kr.astype(BF16).reshape(batch, n, LANES)
    if ckv_ctx is not None:
        kr_pad = jnp.pad(kr_ctx.astype(BF16), ((0, 0), (0, 0), (0, LANES - MLA_ROPE)))
        ckv_all = jnp.concatenate([ckv_all, ckv_ctx.astype(BF16)], axis=1)
        kr_all = jnp.concatenate([kr_all, kr_pad], axis=1)
    nk = ckv_all.shape[1]
    k_cat, v = _mla_kv(ckv_all.reshape(batch * nk, KV_LORA), kr_all.reshape(batch * nk, LANES), w["w_kv"])
    o = _mla_attention(q_cat, k_cat, v, batch)
    y = _matmul([a, o], [w["w_out_a"], w["w_out_o"]], F32, name="ab_out_proj")
    return y, ckv, kr


def _mixer_c(h, w_in, sink, w_out, batch, tables, k_ctx=None, v_ctx=None):
    if tables is None:
        z = _matmul([h], [w_in], F32, name="swa_in_proj")
        o = _swa_context(z, sink, batch)
    else:
        z = _matmul([h], [w_in], BF16, epilogue="rope", tables=tables, rope_shift=HEAD_DIM // 4,
                    rope_cols=C_OUT + C_KV, scale=SWA_SCALE * LOG2E, scale_cols=C_OUT, name="swa_in_proj")
        o = _swa_latent(z, k_ctx, v_ctx, sink, batch)
    return _matmul([o], [w_out], F32, name="swa_out_proj"), z


def _trunk(x, batch, mods, row0, rows_per_mod, prm, caches):
    depth = mods.shape[0]
    n = x.shape[0] // batch
    latent = caches is not None
    tab_mla = _rope_tables(n, MLA_ROPE) if latent else None
    tab_swa = _rope_tables(n, HEAD_DIM) if latent else None
    new = dict(ckv=[], kr=[], k=[], v=[])
    post = functools.partial(_post, rows_per_mod=rows_per_mod, row0=row0)
    h = _pre(x, mods[0], prm["norm_pre"][0, 0], 0, rows_per_mod, row0)
    for l in range(depth):
        j = l // 2
        m = mods[l]
        y = _ffn(h, prm, 2 * l)
        x, h = post(x, y, m, prm["norm_post"][l, 0], 2, MACARON, mods_next=m, g_next=prm["norm_pre"][l, 1],
                    base_next=3)
        if l % 2 == 0:
            ckv_ctx = caches["mla_ckv"][:, j] if latent else None
            kr_ctx = caches["mla_krope"][:, j] if latent else None
            y, ckv, kr = _mixer_ab(h, prm["ab"][j], prm["g_q"][j], prm["g_kv"][j], batch, tab_mla, ckv_ctx, kr_ctx)
            new["ckv"].append(ckv.reshape(batch, n, KV_LORA))
            new["kr"].append(kr[:, :MLA_ROPE].reshape(batch, n, MLA_ROPE))
        else:
            past_k = caches["swa_k"][:, j].reshape(batch, -1, C_KV) if latent else None
            past_v = caches["swa_v"][:, j].reshape(batch, -1, C_KV) if latent else None
            y, z = _mixer_c(h, prm["w_in_c"][j], prm["sink_c"][j], prm["w_out_c"][j], batch, tab_swa, past_k, past_v)
            new["k"].append(z[:, C_OUT:C_OUT + C_KV].reshape(batch, n, SWA_KV_HEADS, HEAD_DIM))
            new["v"].append(z[:, C_OUT + C_KV:].reshape(batch, n, SWA_KV_HEADS, HEAD_DIM))
        x, h = post(x, y, m, prm["norm_post"][l, 1], 5, 1.0, mods_next=m, g_next=prm["norm_pre"][l, 2], base_next=6)
        y = _ffn(h, prm, 2 * l + 1)
        if l + 1 < depth:
            x, h = post(x, y, m, prm["norm_post"][l, 2], 8, MACARON, mods_next=mods[l + 1],
                        g_next=prm["norm_pre"][l + 1, 0], base_next=0)
        else:
            x, _ = post(x, y, m, prm["norm_post"][l, 2], 8, MACARON)
    return x, new


def kernel(x_prompt, x_sample, cache_mla_ckv, cache_mla_krope, cache_swa_k, cache_swa_v, c, c_ctx, w_mod, b_mod,
           norm_pre, norm_post, w_ffn_gate, w_ffn_up, w_ffn_down, w_in_ab, g_q, g_kv, w_uq, w_ukv, w_s, b_s,
           w_out_ab, w_in_c, sink_c, w_out_c):
    depth = w_mod.shape[0]
    batch, seq, d = x_prompt.shape
    dec_batch, dec_seq, _ = x_sample.shape
    assert 1 + dec_batch <= MOD_ROWS

    cond = jnp.concatenate([c_ctx[None, :], c], axis=0)
    cond = jnp.pad(cond, ((0, MOD_ROWS - cond.shape[0]), (0, 0)))
    mods = _ada_mods(cond, w_mod, b_mod)

    ff = w_ffn_gate.shape[-1]
    prm = dict(
        norm_pre=norm_pre, norm_post=norm_post, g_q=g_q, g_kv=g_kv, sink_c=sink_c,
        ffn=(_cast_bf16(w_ffn_gate.reshape(2 * depth, d, ff)), _cast_bf16(w_ffn_up.reshape(2 * depth, d, ff)),
             _cast_bf16(w_ffn_down.reshape(2 * depth, ff, d))),
        ab=[_ab_weights(w_in_ab[j], w_uq[j], w_ukv[j], w_s[j], b_s[j], w_out_ab[j]) for j in range(w_in_ab.shape[0])],
        w_in_c=[w_in_c[j].astype(BF16) for j in range(w_in_c.shape[0])],
        w_out_c=[w_out_c[j].astype(BF16) for j in range(w_out_c.shape[0])],
    )

    y_prompt, new = _trunk(x_prompt.reshape(batch * seq, d), batch, mods, 0, batch * seq, prm, None)
    caches = dict(mla_ckv=cache_mla_ckv, mla_krope=cache_mla_krope, swa_k=cache_swa_k, swa_v=cache_swa_v)
    y_sample, _ = _trunk(x_sample.reshape(dec_batch * dec_seq, d), dec_batch, mods, 1, dec_seq, prm, caches)

    return (y_prompt.reshape(batch, seq, d), y_sample.reshape(dec_batch, dec_seq, d),
            jnp.stack(new["ckv"], axis=1), jnp.stack(new["kr"], axis=1),
            jnp.stack(new["k"], axis=1), jnp.stack(new["v"], axis=1))
```
